```python
import jax, jax.numpy as jnp
from jax import lax
import numpy as np

D_MODEL = 4096
BATCH = 2
SEQ = 4096
DEPTH = 1

GRID_W = 64
CTX_LEN = 256
NA_HEADS = 16
NA_HEAD_DIM = 128
NA_WIN_R = 8
NA_WIN_C = 16
RET_HEADS = 8
RET_QK_DIM = 256
RET_V_DIM = 256
RET_CHUNK = 128
RET_DECAY_BASE_EXP = 5.0
N_EXPERTS = 16
EXPERT_FF = 2048
EC_CAPACITY_FACTOR = 2
ROPE_BASE = 10000.0
NORM_EPS = 1e-6
NEG_INF = -1e30
N_MOD = 6

NA_WIDTH = NA_HEADS * NA_HEAD_DIM
RET_QK_WIDTH = RET_HEADS * RET_QK_DIM
RET_V_WIDTH = RET_HEADS * RET_V_DIM
NA_Q0 = 0
NA_K0 = NA_Q0 + NA_WIDTH
NA_V0 = NA_K0 + NA_WIDTH
RET_Q0 = NA_V0 + NA_WIDTH
RET_K0 = RET_Q0 + RET_QK_WIDTH
RET_V0 = RET_K0 + RET_QK_WIDTH
RET_GF0 = RET_V0 + RET_V_WIDTH
RET_GB0 = RET_GF0 + RET_V_WIDTH
GATE_A0 = RET_GB0 + RET_V_WIDTH
GATE_B0 = GATE_A0 + D_MODEL
D_IN = GATE_B0 + D_MODEL

kernel_name = "hybrid_na_retention_ec_moe_block"


def rmsnorm(x, g):
    xf = x.astype(jnp.float32)
    y = xf * lax.rsqrt(jnp.mean(xf * xf, axis=-1, keepdims=True) + NORM_EPS)
    return (y * g.astype(jnp.float32)).astype(x.dtype)


def head_rmsnorm(o):
    of = o.astype(jnp.float32)
    return of * lax.rsqrt(jnp.mean(of * of, axis=-1, keepdims=True) + NORM_EPS)


def modulate(h, shift, scale):
    return h * (1 + scale) + shift


def heads(a, n):
    b, t, _ = a.shape
    return a.reshape(b, t, n, -1).transpose(0, 2, 1, 3)


def merge_heads(a):
    b, h, t, d = a.shape
    return a.transpose(0, 2, 1, 3).reshape(b, t, h * d)


def split_in(proj):
    bounds = (NA_K0, NA_V0, RET_Q0, RET_K0, RET_V0, RET_GF0, RET_GB0, GATE_A0, GATE_B0)
    return jnp.split(proj, bounds, axis=-1)


def rope_1d(a, pos):
    half = a.shape[-1] // 2
    inv = ROPE_BASE ** (-jnp.arange(half, dtype=jnp.float32) / half)
    ang = pos[:, None] * inv[None, :]
    cos, sin = jnp.cos(ang), jnp.sin(ang)
    a1, a2 = a[..., :half], a[..., half:]
    return jnp.concatenate([a1 * cos - a2 * sin, a1 * sin + a2 * cos], axis=-1).astype(a.dtype)


def axial_rope(a, row_pos, col_pos):
    half = a.shape[-1] // 2
    return jnp.concatenate([rope_1d(a[..., :half], row_pos), rope_1d(a[..., half:], col_pos)], axis=-1)


def neighbourhood_attention(q, k, v, k_ctx, v_ctx, rpb):
    b, h, t, dh = q.shape
    rows = t // GRID_W
    wr = min(NA_WIN_R, rows)
    scale = dh ** -0.5
    qg = q.reshape(b, h, rows, GRID_W, dh)
    kg = k.reshape(b, h, rows, GRID_W, dh)
    vg = v.reshape(b, h, rows, GRID_W, dh)
    r = jnp.arange(rows)
    r0 = jnp.clip(r - wr // 2, 0, rows - wr)
    key_rows = r0[:, None] + jnp.arange(wr)[None, :]
    k_win = kg[:, :, key_rows]
    v_win = vg[:, :, key_rows]
    cidx = jnp.arange(GRID_W)
    c0 = jnp.clip(cidx - NA_WIN_C // 2, 0, GRID_W - NA_WIN_C)
    col_ok = (cidx[None, :] >= c0[:, None]) & (cidx[None, :] < c0[:, None] + NA_WIN_C)
    dr = key_rows - r[:, None] + (NA_WIN_R - 1)
    dc = jnp.clip(cidx[None, :] - cidx[:, None], -(NA_WIN_C - 1), NA_WIN_C - 1) + (NA_WIN_C - 1)
    bias = rpb[:, dr[:, None, :, None], dc[None, :, None, :]]
    s_win = jnp.einsum('bhrqd,bhrwkd->bhrqwk', qg, k_win).astype(jnp.float32) * scale + bias[None].astype(jnp.float32)
    s_win = jnp.where(col_ok[:, None, :], s_win, NEG_INF)
    n_win = wr * GRID_W
    s_win = s_win.reshape(b, h, rows, GRID_W, n_win)
    s_ctx = jnp.einsum('bhrqd,bhcd->bhrqc', qg, k_ctx).astype(jnp.float32) * scale
    p = jax.nn.softmax(jnp.concatenate([s_win, s_ctx], axis=-1), axis=-1)
    p_win = p[..., :n_win].reshape(b, h, rows, GRID_W, wr, GRID_W).astype(v.dtype)
    p_ctx = p[..., n_win:].astype(v.dtype)
    out = jnp.einsum('bhrqwk,bhrwkd->bhrqd', p_win, v_win) + jnp.einsum('bhrqc,bhcd->bhrqd', p_ctx, v_ctx)
    return out.reshape(b, h, t, dh)


def context_attention(q, k, v):
    s = jnp.einsum('bhqd,bhkd->bhqk', q, k).astype(jnp.float32) * (q.shape[-1] ** -0.5)
    p = jax.nn.softmax(s, axis=-1).astype(v.dtype)
    return jnp.einsum('bhqk,bhkd->bhqd', p, v)


def retention_log_decay(decay_exp):
    return jnp.log1p(-jnp.exp2(-decay_exp.astype(jnp.float32)))


def context_final_state(k, v, log_gamma, reverse):
    tc = k.shape[2]
    t = jnp.arange(tc, dtype=jnp.float32)
    steps = t if reverse else (tc - 1.0 - t)
    w = jnp.exp(log_gamma[:, None] * steps[None, :])
    return jnp.einsum('bhtk,bhtv,ht->bhkv', k.astype(jnp.float32), v.astype(jnp.float32), w)


def chunk_retention(q, k, v, log_gamma, s0):
    b, h, t, _ = q.shape
    dv = v.shape[-1]
    n = t // RET_CHUNK
    pos = jnp.arange(RET_CHUNK, dtype=jnp.float32)
    diff = pos[:, None] - pos[None, :]
    decay_in = jnp.where(diff >= 0, jnp.exp(log_gamma[:, None, None] * jnp.maximum(diff, 0.0)), 0.0)
    q_dec = jnp.exp(log_gamma[:, None] * (pos + 1.0))[..., None]
    k_dec = jnp.exp(log_gamma[:, None] * (RET_CHUNK - 1.0 - pos))[..., None]
    chunk_dec = jnp.exp(log_gamma * RET_CHUNK)[:, None, None]

    def to_chunks(a):
        return jnp.moveaxis(a.reshape(b, h, n, RET_CHUNK, a.shape[-1]), 2, 0)

    def step(s, inp):
        qb, kb, vb = inp
        scores = jnp.einsum('bhid,bhjd->bhij', qb, kb) * decay_in
        o = jnp.einsum('bhij,bhjv->bhiv', scores, vb) + jnp.einsum('bhid,bhdv->bhiv', qb, s) * q_dec
        s = s * chunk_dec + jnp.einsum('bhjd,bhjv->bhdv', kb * k_dec, vb)
        return s, o

    _, out = lax.scan(step, s0.astype(jnp.float32), (to_chunks(q), to_chunks(k), to_chunks(v)))
    return jnp.moveaxis(out, 0, 2).reshape(b, h, t, dv)


def bidirectional_retention(q, k, v, log_gamma, s_f, s_b):
    rev = lambda a: jnp.flip(a, axis=2)
    o_f = chunk_retention(q, k, v, log_gamma[0], s_f)
    o_b = rev(chunk_retention(rev(q), rev(k), rev(v), log_gamma[1], s_b))
    return o_f, o_b


def merge_branches(y_na, o_f, o_b, g_rf, g_rb, gate_a, gate_b, w_branch_na, w_branch_ret, w_out):
    y_ret = (jax.nn.silu(g_rf) * merge_heads(head_rmsnorm(o_f))
             + jax.nn.silu(g_rb) * merge_heads(head_rmsnorm(o_b))).astype(y_na.dtype)
    mixed = jax.nn.sigmoid(gate_a) * (y_na @ w_branch_na) + jax.nn.sigmoid(gate_b) * (y_ret @ w_branch_ret)
    return mixed @ w_out


def expert_choice_ffn(h, w_router, w_gate, w_up, w_down):
    b, t, d = h.shape
    cap = EC_CAPACITY_FACTOR * t // N_EXPERTS
    aff = jax.nn.softmax((h @ w_router).astype(jnp.float32), axis=-1)
    g, idx = lax.top_k(aff.transpose(0, 2, 1), cap)
    xe = jax.vmap(lambda hb, ib: hb[ib])(h, idx)
    a = jnp.einsum('becd,edf->becf', xe, w_gate)
    u = jnp.einsum('becd,edf->becf', xe, w_up)
    ye = jnp.einsum('becf,efd->becd', jax.nn.silu(a) * u, w_down) * g[..., None].astype(h.dtype)
    return jax.vmap(lambda yb, ib: jnp.zeros((t, d), yb.dtype).at[ib.reshape(-1)].add(yb.reshape(-1, d)))(ye, idx)


def hybrid_layer(x, cx, c, c_ctx, norm1, norm2, w_mod, b_mod, w_in, na_rpb, ret_decay,
                 w_branch_na, w_branch_ret, w_out, w_router, w_gate, w_up, w_down, need_ctx_out):
    b, t, _ = x.shape
    mod = jax.nn.silu(c) @ w_mod + b_mod
    mod_c = jax.nn.silu(c_ctx) @ w_mod + b_mod
    sh1, sc1, g1, sh2, sc2, g2 = [m[:, None, :] for m in jnp.split(mod, N_MOD, axis=-1)]
    csh1, csc1, cg1, csh2, csc2, cg2 = jnp.split(mod_c, N_MOD, axis=-1)

    h = modulate(rmsnorm(x, norm1), sh1, sc1)
    hc = modulate(rmsnorm(cx, norm1), csh1, csc1)
    q_a, k_a, v_a, q_r, k_r, v_r, g_rf, g_rb, gate_a, gate_b = split_in(h @ w_in)
    if need_ctx_out:
        q_ac, k_ac, v_ac, q_rc, k_rc, v_rc, g_rfc, g_rbc, gate_ac, gate_bc = split_in(hc @ w_in)
    else:
        k_ac, v_ac = jnp.split(hc @ w_in[:, NA_K0:RET_Q0], 2, axis=-1)
        k_rc, v_rc = jnp.split(hc @ w_in[:, RET_K0:RET_GF0], (RET_QK_WIDTH,), axis=-1)

    ka_c, va_c = heads(k_ac, NA_HEADS), heads(v_ac, NA_HEADS)
    y_na = merge_heads(neighbourhood_attention(heads(q_a, NA_HEADS), heads(k_a, NA_HEADS), heads(v_a, NA_HEADS),
                                               ka_c, va_c, na_rpb))

    log_gamma = retention_log_decay(ret_decay)
    tpos = jnp.arange(t)
    row_pos = (tpos // GRID_W).astype(jnp.float32)
    col_pos = (tpos % GRID_W).astype(jnp.float32)
    k_scale = RET_QK_DIM ** -0.5
    qr = axial_rope(heads(q_r, RET_HEADS), row_pos, col_pos)
    kr = axial_rope(heads(k_r, RET_HEADS), row_pos, col_pos) * k_scale
    vr = heads(v_r, RET_HEADS)
    krc = heads(k_rc, RET_HEADS) * k_scale
    vrc = heads(v_rc, RET_HEADS)
    s_f = context_final_state(krc, vrc, log_gamma[0], reverse=False)
    s_b = context_final_state(krc, vrc, log_gamma[1], reverse=True)
    o_f, o_b = bidirectional_retention(qr, kr, vr, log_gamma, s_f, s_b)

    x = x + g1 * merge_branches(y_na, o_f, o_b, g_rf, g_rb, gate_a, gate_b, w_branch_na, w_branch_ret, w_out)

    h2 = modulate(rmsnorm(x, norm2), sh2, sc2)
    x = x + g2 * expert_choice_ffn(h2, w_router, w_gate, w_up, w_down)

    if need_ctx_out:
        yc_na = merge_heads(context_attention(heads(q_ac, NA_HEADS), ka_c, va_c))
        zero_state = jnp.zeros_like(s_f)
        oc_f, oc_b = bidirectional_retention(heads(q_rc, RET_HEADS), krc, vrc, log_gamma, zero_state, zero_state)
        cx = cx + cg1 * merge_branches(yc_na, oc_f, oc_b, g_rfc, g_rbc, gate_ac, gate_bc,
                                       w_branch_na, w_branch_ret, w_out)
        hc2 = modulate(rmsnorm(cx, norm2), csh2, csc2)
        cx = cx + cg2 * expert_choice_ffn(hc2, w_router, w_gate, w_up, w_down)
    return x, cx


def setup_inputs(seed: int = 0) -> dict:
    key = jax.random.key(seed)
    ks = jax.random.split(key, 20)
    f32 = jnp.float32

    def nrm(k, shape, scale):
        return jax.random.normal(k, shape, f32) * scale

    D = D_MODEL
    return {
        "x": nrm(ks[0], (BATCH, SEQ, D), 1.0),
        "c": nrm(ks[1], (BATCH, D), 1.0),
        "ctx": nrm(ks[2], (BATCH, CTX_LEN, D), 1.0),
        "c_ctx": nrm(ks[3], (D,), 1.0),
        "norm1": 1.0 + nrm(ks[4], (DEPTH, D), 0.05),
        "norm2": 1.0 + nrm(ks[5], (DEPTH, D), 0.05),
        "w_mod": nrm(ks[6], (DEPTH, D, N_MOD * D), 0.5 * D ** -0.5),
        "b_mod": nrm(ks[7], (DEPTH, N_MOD * D), 0.02),
        "w_in": nrm(ks[8], (DEPTH, D, D_IN), D ** -0.5),
        "na_rpb": nrm(ks[9], (DEPTH, NA_HEADS, 2 * NA_WIN_R - 1, 2 * NA_WIN_C - 1), 0.1),
        "ret_decay": RET_DECAY_BASE_EXP + jnp.arange(RET_HEADS, dtype=f32)[None, None, :]
                     + nrm(ks[10], (DEPTH, 2, RET_HEADS), 0.1),
        "w_branch_na": nrm(ks[11], (DEPTH, NA_WIDTH, D), NA_WIDTH ** -0.5),
        "w_branch_ret": nrm(ks[12], (DEPTH, RET_V_WIDTH, D), RET_V_WIDTH ** -0.5),
        "w_out": nrm(ks[13], (DEPTH, D, D), D ** -0.5),
        "w_router": nrm(ks[14], (DEPTH, D, N_EXPERTS), D ** -0.5),
        "w_gate": nrm(ks[15], (DEPTH, N_EXPERTS, D, EXPERT_FF), D ** -0.5),
        "w_up": nrm(ks[16], (DEPTH, N_EXPERTS, D, EXPERT_FF), D ** -0.5),
        "w_down": nrm(ks[17], (DEPTH, N_EXPERTS, EXPERT_FF, D), EXPERT_FF ** -0.5),
        "final_norm": 1.0 + nrm(ks[18], (D,), 0.05),
    }


def reference(x, c, ctx, c_ctx, norm1, norm2, w_mod, b_mod, w_in, na_rpb, ret_decay,
              w_branch_na, w_branch_ret, w_out, w_router, w_gate, w_up, w_down, final_norm):
    for layer in range(DEPTH):
        x, ctx = hybrid_layer(x, ctx, c, c_ctx, norm1[layer], norm2[layer], w_mod[layer], b_mod[layer],
                              w_in[layer], na_rpb[layer], ret_decay[layer], w_branch_na[layer],
                              w_branch_ret[layer], w_out[layer], w_router[layer], w_gate[layer],
                              w_up[layer], w_down[layer], need_ctx_out=layer + 1 < DEPTH)
    return rmsnorm(x, final_norm)
```

```python
import functools

import numpy as np
import jax
import jax.numpy as jnp
from jax import lax
from jax.experimental import pallas as pl
from jax.experimental.pallas import tpu as pltpu

F32 = jnp.float32
BF16 = jnp.bfloat16

GRID_W = 64
NA_HEADS = 16
NA_HEAD_DIM = 128
NA_WIN_R = 8
NA_WIN_C = 16
RET_HEADS = 8
RET_DIM = 256
RET_CHUNK = 256
N_EXPERTS = 16
EC_CAPACITY_FACTOR = 2
ROPE_BASE = 10000.0
NORM_EPS = 1e-6
NEG_INF = -1e30
N_MOD = 6

VMEM_LIMIT_BYTES = 56 * 1024 * 1024
LANES = 128

NA_Q_ROWS = 4
NA_K_ROWS = NA_Q_ROWS + NA_WIN_R


def _params(*sem):
    return pltpu.CompilerParams(dimension_semantics=sem, vmem_limit_bytes=VMEM_LIMIT_BYTES)


def _dot(a, b):
    return jnp.dot(a, b, preferred_element_type=F32)


def _dot_nt(a, b):
    return lax.dot_general(a, b, (((1,), (1,)), ((), ())), preferred_element_type=F32)


def _dot_tn(a, b):
    return lax.dot_general(a, b, (((0,), (0,)), ((), ())), preferred_element_type=F32)


def _silu(x):
    return x * jax.nn.sigmoid(x)


def _mod_kernel(c_ref, w_ref, b_ref, o_ref):
    a = _silu(c_ref[...]).astype(BF16)
    o_ref[...] = _dot(a, w_ref[...].astype(BF16)) + b_ref[...]


def _modulation(cvec, w_mod, b_mod):
    r, d = cvec.shape
    n = w_mod.shape[1]
    tn = 512
    return pl.pallas_call(
        _mod_kernel,
        out_shape=jax.ShapeDtypeStruct((r, n), F32),
        grid=(n // tn,),
        in_specs=[pl.BlockSpec((r, d), lambda j: (0, 0)),
                  pl.BlockSpec((d, tn), lambda j: (0, j)),
                  pl.BlockSpec((1, tn), lambda j: (0, j))],
        out_specs=pl.BlockSpec((r, tn), lambda j: (0, j)),
        compiler_params=_params("arbitrary"),
        name="modulation",
    )(cvec, w_mod, b_mod.reshape(1, n))


def _modulated_norm(x, g, shift, scale):
    y = x * lax.rsqrt(jnp.mean(x * x, axis=-1, keepdims=True) + NORM_EPS)
    return (y * g) * (1.0 + scale) + shift


def _prenorm_kernel(x_ref, g_ref, sh_ref, sc_ref, o_ref):
    o_ref[...] = _modulated_norm(x_ref[...], g_ref[...], sh_ref[...], sc_ref[...]).astype(o_ref.dtype)


def _prenorm(x2d, gain, mod4, rows_per_sample, sample0, k_shift):
    r, d = x2d.shape
    tr = 256
    per = rows_per_sample // tr
    return pl.pallas_call(
        _prenorm_kernel,
        out_shape=jax.ShapeDtypeStruct((r, d), BF16),
        grid=(r // tr,),
        in_specs=[pl.BlockSpec((tr, d), lambda i: (i, 0)),
                  pl.BlockSpec((1, d), lambda i: (0, 0)),
                  pl.BlockSpec((None, None, 1, d), lambda i: (sample0 + i // per, k_shift, 0, 0)),
                  pl.BlockSpec((None, None, 1, d), lambda i: (sample0 + i // per, k_shift + 1, 0, 0))],
        out_specs=pl.BlockSpec((tr, d), lambda i: (i, 0)),
        compiler_params=_params("parallel"),
        name="prenorm",
    )(x2d, gain.reshape(1, d), mod4, mod4)


def _mm_kernel(a_ref, w_ref, o_ref):
    o_ref[...] = _dot(a_ref[...], w_ref[...].astype(BF16)).astype(o_ref.dtype)


def _matmul(a, w, n_out, tm, tn, col_block, name):
    m, k = a.shape
    return pl.pallas_call(
        _mm_kernel,
        out_shape=jax.ShapeDtypeStruct((m, n_out), BF16),
        grid=(m // tm, n_out // tn),
        in_specs=[pl.BlockSpec((tm, k), lambda i, j: (i, 0), pipeline_mode=pl.Buffered(1)),
                  pl.BlockSpec((k, tn), lambda i, j: (0, col_block(j)))],
        out_specs=pl.BlockSpec((tm, tn), lambda i, j: (i, j)),
        compiler_params=_params("parallel", "arbitrary"),
        name=name,
    )(a, w)


def _na_tables(rows):
    wr = NA_WIN_R
    bases, tables = [], []
    for t in range(rows // NA_Q_ROWS):
        kb = int(np.clip(NA_Q_ROWS * t - wr // 2, 0, rows - NA_K_ROWS))
        tab = []
        for i in range(NA_Q_ROWS):
            r = NA_Q_ROWS * t + i
            r0 = int(np.clip(r - wr // 2, 0, rows - wr))
            tab.append(tuple((kb + j - r + NA_WIN_R - 1) if r0 <= kb + j < r0 + wr else None
                             for j in range(NA_K_ROWS)))
        bases.append(kb)
        tables.append(tuple(tab))
    uniq = list(dict.fromkeys(tables))
    return bases, uniq, [uniq.index(t) for t in tables]


def _na_bias(rpb, patterns):
    h = rpb.shape[0]
    cidx = np.arange(GRID_W)
    c0 = np.clip(cidx - NA_WIN_C // 2, 0, GRID_W - NA_WIN_C)
    col_ok = (cidx[None, :] >= c0[:, None]) & (cidx[None, :] < c0[:, None] + NA_WIN_C)
    dc = np.clip(cidx[None, :] - cidx[:, None], -(NA_WIN_C - 1), NA_WIN_C - 1) + (NA_WIN_C - 1)
    tile = jnp.where(col_ok[None, None], rpb.astype(F32)[:, :, dc], NEG_INF)
    neg = jnp.full((h, GRID_W, GRID_W), NEG_INF, F32)
    pats = []
    for tab in patterns:
        pats.append(jnp.concatenate(
            [jnp.concatenate([neg if dr is None else tile[:, dr] for dr in row], axis=2) for row in tab], axis=1))
    return jnp.stack(pats)


def _na_kernel(q_ref, k_ref, v_ref, kc_ref, vc_ref, bias_ref, o_ref, *, rows, pat_ids):
    tq = NA_Q_ROWS * GRID_W
    nk = NA_K_ROWS * GRID_W
    scale = NA_HEAD_DIM ** -0.5

    def tile(t, carry):
        kb = pl.multiple_of(jnp.clip(NA_Q_ROWS * t - NA_WIN_R // 2, 0, rows - NA_K_ROWS) * GRID_W, tq)
        q0 = pl.multiple_of(t * tq, tq)
        pat = 0
        for i, pid in enumerate(pat_ids):
            pat = jnp.where(t == i, pid, pat)
        q = q_ref[pl.ds(q0, tq), :]
        s_w = _dot_nt(q, k_ref[pl.ds(kb, nk), :]) * scale + bias_ref[pat]
        s_c = _dot_nt(q, kc_ref[...]) * scale
        m = jnp.maximum(jnp.max(s_w, axis=-1, keepdims=True), jnp.max(s_c, axis=-1, keepdims=True))
        p_w = jnp.exp(s_w - m)
        p_c = jnp.exp(s_c - m)
        l = jnp.sum(p_w, axis=-1, keepdims=True) + jnp.sum(p_c, axis=-1, keepdims=True)
        o = _dot(p_w.astype(BF16), v_ref[pl.ds(kb, nk), :]) + _dot(p_c.astype(BF16), vc_ref[...])
        o_ref[pl.ds(q0, tq), :] = (o / l).astype(o_ref.dtype)
        return carry

    lax.fori_loop(0, rows // NA_Q_ROWS, tile, 0, unroll=2)


def _na_attention(proj, cproj, rpb, batch, t_len, tc_len, col_q, col_k, col_v):
    rows = t_len // GRID_W
    tq = NA_Q_ROWS * GRID_W
    _, patterns, pat_ids = _na_tables(rows)
    bias = _na_bias(rpb, patterns)
    dh = NA_HEAD_DIM
    tok = lambda col: pl.BlockSpec((t_len, dh), lambda b, h: (b, col // dh + h))
    return pl.pallas_call(
        functools.partial(_na_kernel, rows=rows, pat_ids=tuple(pat_ids)),
        out_shape=jax.ShapeDtypeStruct((batch * t_len, NA_HEADS * dh), BF16),
        grid=(batch, NA_HEADS),
        in_specs=[tok(col_q), tok(col_k), tok(col_v),
                  pl.BlockSpec((tc_len, dh), lambda b, h: (b, h)),
                  pl.BlockSpec((tc_len, dh), lambda b, h: (b, NA_HEADS + h)),
                  pl.BlockSpec((len(patterns), None, tq, NA_K_ROWS * GRID_W), lambda b, h: (0, h, 0, 0))],
        out_specs=pl.BlockSpec((t_len, dh), lambda b, h: (b, h)),
        compiler_params=_params("parallel", "arbitrary"),
        name="na_attention",
    )(proj, proj, proj, cproj, cproj, bias)


def _rope_tables(t_len):
    quarter = RET_DIM // 4
    inv = ROPE_BASE ** (-jnp.arange(quarter, dtype=F32) / quarter)
    tpos = jnp.arange(t_len)
    row_ang = (tpos // GRID_W).astype(F32)[:, None] * inv[None, :]
    col_ang = (tpos % GRID_W).astype(F32)[:, None] * inv[None, :]
    cos = jnp.concatenate([jnp.cos(row_ang)] * 2 + [jnp.cos(col_ang)] * 2, axis=-1)
    sin = jnp.concatenate([-jnp.sin(row_ang), jnp.sin(row_ang), -jnp.sin(col_ang), jnp.sin(col_ang)], axis=-1)
    return cos, sin


def _rope(a, cos, sin):
    half = RET_DIM // 2
    swapped = jnp.concatenate([pltpu.roll(a[:, :half], half // 2, 1), pltpu.roll(a[:, half:], half // 2, 1)], axis=1)
    return a * cos + swapped * sin


def _ret_kernel(dec_ref, q_ref, k_ref, v_ref, gf_ref, gb_ref, kc_ref, vc_ref, cos_ref, sin_ref, o_ref,
                sf_ref, sb_ref, acc_ref):
    h = pl.program_id(1)
    c = RET_CHUNK
    t_len = q_ref.shape[0]
    tc_len = kc_ref.shape[0]
    nc = t_len // c
    k_scale = RET_DIM ** -0.5

    def log_gamma(direction):
        e = jnp.full((1, 1), dec_ref[direction, h], F32)
        return jnp.log1p(-jnp.exp2(-e))

    lg_f, lg_b = log_gamma(0), log_gamma(1)
    pos = lax.broadcasted_iota(jnp.int32, (c, 1), 0).astype(F32)
    diff = pos - lax.broadcasted_iota(jnp.int32, (1, c), 1).astype(F32)
    dec_f = jnp.where(diff >= 0, jnp.exp(lg_f * jnp.maximum(diff, 0.0)), 0.0)
    dec_b = jnp.where(diff <= 0, jnp.exp(lg_b * jnp.maximum(-diff, 0.0)), 0.0)
    qdec_f, kdec_f, cdec_f = jnp.exp(lg_f * (pos + 1.0)), jnp.exp(lg_f * (c - 1.0 - pos)), jnp.exp(lg_f * c)
    qdec_b, kdec_b, cdec_b = jnp.exp(lg_b * (c - pos)), jnp.exp(lg_b * pos), jnp.exp(lg_b * c)

    cpos = lax.broadcasted_iota(jnp.int32, (tc_len, 1), 0).astype(F32)
    kc = kc_ref[...].astype(F32) * k_scale
    vc = vc_ref[...]
    sf_ref[...] = _dot_tn((kc * jnp.exp(lg_f * (tc_len - 1.0 - cpos))).astype(BF16), vc)
    sb_ref[...] = _dot_tn((kc * jnp.exp(lg_b * cpos)).astype(BF16), vc)
    acc_ref[...] = jnp.zeros_like(acc_ref)

    def chunk(n, s_ref, dec, qdec, kdec, cdec, g_ref):
        r0 = pl.multiple_of(n * c, c)
        cos = cos_ref[pl.ds(r0, c), :]
        sin = sin_ref[pl.ds(r0, c), :]
        q = _rope(q_ref[pl.ds(r0, c), :].astype(F32), cos, sin)
        k = _rope(k_ref[pl.ds(r0, c), :].astype(F32), cos, sin) * k_scale
        v = v_ref[pl.ds(r0, c), :]
        qb = q.astype(BF16)
        scores = _dot_nt(qb, k.astype(BF16)) * dec
        s = s_ref[...]
        o = _dot(scores.astype(BF16), v) + _dot(qb, s.astype(BF16)) * qdec
        s_ref[...] = s * cdec + _dot_tn((k * kdec).astype(BF16), v)
        on = o * lax.rsqrt(jnp.mean(o * o, axis=-1, keepdims=True) + NORM_EPS)
        acc_ref[pl.ds(r0, c), :] += _silu(g_ref[pl.ds(r0, c), :].astype(F32)) * on

    def body(n, carry):
        chunk(n, sf_ref, dec_f, qdec_f, kdec_f, cdec_f, gf_ref)
        chunk(nc - 1 - n, sb_ref, dec_b, qdec_b, kdec_b, cdec_b, gb_ref)
        return carry

    lax.fori_loop(0, nc, body, 0)
    o_ref[...] = acc_ref[...].astype(o_ref.dtype)


def _retention(proj, cproj, ret_decay, batch, t_len, tc_len, col_q, col_k, col_v, col_gf, col_gb, ccol_k, ccol_v):
    d = RET_DIM
    cos, sin = _rope_tables(t_len)
    tok = lambda col: pl.BlockSpec((t_len, d), lambda b, h: (b, col // d + h))
    ctx = lambda col: pl.BlockSpec((tc_len, d), lambda b, h: (b, col // d + h))
    tab = pl.BlockSpec((t_len, d), lambda b, h: (0, 0))
    return pl.pallas_call(
        _ret_kernel,
        out_shape=jax.ShapeDtypeStruct((batch * t_len, RET_HEADS * d), BF16),
        grid=(batch, RET_HEADS),
        in_specs=[pl.BlockSpec(memory_space=pltpu.SMEM),
                  tok(col_q), tok(col_k), tok(col_v), tok(col_gf), tok(col_gb), ctx(ccol_k), ctx(ccol_v), tab, tab],
        out_specs=pl.BlockSpec((t_len, d), lambda b, h: (b, h)),
        scratch_shapes=[pltpu.VMEM((d, d), F32), pltpu.VMEM((d, d), F32), pltpu.VMEM((t_len, d), F32)],
        compiler_params=_params("parallel", "arbitrary"),
        name="retention",
    )(ret_decay.astype(F32), proj, proj, proj, proj, proj, cproj, cproj, cos, sin)


def _merge_kernel(ya_ref, yr_ref, wa_ref, wr_ref, ga_ref, gb_ref, o_ref):
    a = _dot(ya_ref[...], wa_ref[...].astype(BF16))
    r = _dot(yr_ref[...], wr_ref[...].astype(BF16))
    o_ref[...] = (jax.nn.sigmoid(ga_ref[...].astype(F32)) * a + jax.nn.sigmoid(gb_ref[...].astype(F32)) * r
                  ).astype(o_ref.dtype)


def _merge(y_na, y_ret, w_na, w_ret, proj, col_ga, col_gb):
    m, ka = y_na.shape
    kr = y_ret.shape[1]
    n = w_na.shape[1]
    tm, tn = 1024, 512
    return pl.pallas_call(
        _merge_kernel,
        out_shape=jax.ShapeDtypeStruct((m, n), BF16),
        grid=(m // tm, n // tn),
        in_specs=[pl.BlockSpec((tm, ka), lambda i, j: (i, 0)),
                  pl.BlockSpec((tm, kr), lambda i, j: (i, 0)),
                  pl.BlockSpec((ka, tn), lambda i, j: (0, j)),
                  pl.BlockSpec((kr, tn), lambda i, j: (0, j)),
                  pl.BlockSpec((tm, tn), lambda i, j: (i, col_ga // tn + j)),
                  pl.BlockSpec((tm, tn), lambda i, j: (i, col_gb // tn + j))],
        out_specs=pl.BlockSpec((tm, tn), lambda i, j: (i, j)),
        compiler_params=_params("parallel", "arbitrary"),
        name="merge",
    )(y_na, y_ret, w_na, w_ret, proj, proj)


def _outproj_kernel(m_ref, w_ref, x_ref, g_ref, o_ref):
    o_ref[...] = x_ref[...] + g_ref[...] * _dot(m_ref[...], w_ref[...].astype(BF16))


def _outproj(mixed, w_out, x2d, mod4, rows_per_sample, k_gate):
    m, k = mixed.shape
    n = w_out.shape[1]
    tm, tn = 1024, 512
    per = rows_per_sample // tm
    return pl.pallas_call(
        _outproj_kernel,
        out_shape=jax.ShapeDtypeStruct((m, n), F32),
        grid=(m // tm, n // tn),
        in_specs=[pl.BlockSpec((tm, k), lambda i, j: (i, 0)),
                  pl.BlockSpec((k, tn), lambda i, j: (0, j)),
                  pl.BlockSpec((tm, tn), lambda i, j: (i, j)),
                  pl.BlockSpec((None, None, 1, tn), lambda i, j: (i // per, k_gate, 0, j))],
        out_specs=pl.BlockSpec((tm, tn), lambda i, j: (i, j)),
        compiler_params=_params("parallel", "arbitrary"),
        name="outproj",
    )(mixed, w_out, x2d, mod4)


def _router_kernel(x_ref, g_ref, sh_ref, sc_ref, wr_ref, o_ref, a_ref):
    d = x_ref.shape[1]
    h = _modulated_norm(x_ref[...], g_ref[...], sh_ref[...], sc_ref[...])
    logits = _dot(h.astype(BF16), wr_ref[...])
    lane = lax.broadcasted_iota(jnp.int32, logits.shape, 1)
    logits = jnp.where(lane < N_EXPERTS, logits, NEG_INF)
    p = jnp.exp(logits - jnp.max(logits, axis=-1, keepdims=True))
    aff = p / jnp.sum(p, axis=-1, keepdims=True)
    o_ref[:, :d] = h
    o_ref[:, d:] = aff
    a_ref[...] = aff[:, :N_EXPERTS]


def _router(x2d, gain, mod4, w_router_pad, rows_per_sample, k_shift):
    r, d = x2d.shape
    tr = 256
    per = rows_per_sample // tr
    return pl.pallas_call(
        _router_kernel,
        out_shape=(jax.ShapeDtypeStruct((r, d + LANES), F32), jax.ShapeDtypeStruct((r, N_EXPERTS), F32)),
        grid=(r // tr,),
        in_specs=[pl.BlockSpec((tr, d), lambda i: (i, 0)),
                  pl.BlockSpec((1, d), lambda i: (0, 0)),
                  pl.BlockSpec((None, None, 1, d), lambda i: (i // per, k_shift, 0, 0)),
                  pl.BlockSpec((None, None, 1, d), lambda i: (i // per, k_shift + 1, 0, 0)),
                  pl.BlockSpec((d, LANES), lambda i: (0, 0))],
        out_specs=(pl.BlockSpec((tr, d + LANES), lambda i: (i, 0)), pl.BlockSpec((tr, N_EXPERTS), lambda i: (i, 0))),
        compiler_params=_params("parallel"),
        name="router",
    )(x2d, gain.reshape(1, d), mod4, mod4, w_router_pad)


TOPK_TILE = 256
BISECT_STEPS = 160


def _topk_kernel(aff_ref, affc_ref, slot_ref, idx_ref, bounds_ref, *, cap):
    t_len = aff_ref.shape[0]
    tt = TOPK_TILE
    nt = t_len // tt
    packed = affc_ref[...]

    def per_expert(v):
        shift = LANES // 2
        while shift >= N_EXPERTS:
            v = v + pltpu.roll(v, shift, 1)
            shift //= 2
        return v

    def bisect(_, c):
        lo, hi = c
        mid = 0.5 * (lo + hi)
        ge = per_expert(jnp.sum(jnp.where(packed >= mid, 1.0, 0.0), axis=0, keepdims=True)) >= cap
        return jnp.where(ge, mid, lo), jnp.where(ge, hi, mid)

    lo, hi = lax.fori_loop(0, BISECT_STEPS, bisect, (jnp.zeros((1, LANES), F32), jnp.full((1, LANES), 2.0, F32)))

    def count_above(i, cnt):
        r0 = pl.multiple_of(i * tt, tt)
        return cnt + jnp.sum(jnp.where(aff_ref[pl.ds(r0, tt), :] >= hi, 1.0, 0.0), axis=0, keepdims=True)
    need = cap - lax.fori_loop(0, nt, count_above, jnp.zeros((1, LANES), F32))

    tri = jnp.where(lax.broadcasted_iota(jnp.int32, (tt, tt), 0) >= lax.broadcasted_iota(jnp.int32, (tt, tt), 1),
                    1.0, 0.0).astype(BF16)

    def assign(i, carry):
        eq_before, sel_before = carry
        r0 = pl.multiple_of(i * tt, tt)
        a = aff_ref[pl.ds(r0, tt), :]
        above = a >= hi
        tie = (a >= lo) & (a < hi)
        eq = jnp.where(tie, 1.0, 0.0)
        eq_rank = _dot(tri, eq.astype(BF16)) + eq_before
        sel = jnp.where(above | (tie & (eq_rank <= need)), 1.0, 0.0)
        sel_rank = _dot(tri, sel.astype(BF16)) + sel_before
        slot_ref[pl.ds(r0, tt), :] = jnp.where(sel > 0, sel_rank - 1.0, -1.0).astype(jnp.int32)
        bounds_ref[pl.ds(i, 1), :] = sel_before.astype(jnp.int32)
        return (eq_before + jnp.sum(eq, axis=0, keepdims=True), sel_before + jnp.sum(sel, axis=0, keepdims=True))

    zero = jnp.zeros((1, LANES), F32)
    _, total = lax.fori_loop(0, nt, assign, (zero, zero))
    bounds_ref[nt:nt + 1, :] = total.astype(jnp.int32)

    sub = lax.broadcasted_iota(jnp.int32, (8, tt), 0)
    lane_tok = lax.broadcasted_iota(jnp.int32, (8, tt), 1)
    slot_iota = lax.broadcasted_iota(jnp.int32, (1, cap), 1)
    for e in range(N_EXPERTS):
        def body(i, acc):
            r0 = pl.multiple_of(i * tt, tt)
            tok = lane_tok + r0
            parts = jnp.where(sub == 0, tok >> 6, jnp.where(sub == 1, tok & 63, 0)).astype(F32).astype(BF16)
            onehot = jnp.where(slot_ref[pl.ds(r0, tt), e:e + 1] == slot_iota, 1.0, 0.0).astype(BF16)
            return acc + _dot(parts, onehot)
        acc = lax.fori_loop(0, nt, body, jnp.zeros((8, cap), F32))
        idx_ref[e:e + 1, :] = (acc[0:1] * 64.0 + acc[1:2]).astype(jnp.int32)


def _topk(hext, aff, batch, t_len, d, cap):
    nb = t_len // TOPK_TILE + 1
    packed_rows = t_len * N_EXPERTS // LANES
    return pl.pallas_call(
        functools.partial(_topk_kernel, cap=cap),
        out_shape=(jax.ShapeDtypeStruct((batch * t_len, LANES), jnp.int32),
                   jax.ShapeDtypeStruct((batch, N_EXPERTS, cap), jnp.int32),
                   jax.ShapeDtypeStruct((batch, nb, LANES), jnp.int32)),
        grid=(batch,),
        in_specs=[pl.BlockSpec((t_len, LANES), lambda b: (b, d // LANES)),
                  pl.BlockSpec((None, packed_rows, LANES), lambda b: (b, 0, 0))],
        out_specs=(pl.BlockSpec((t_len, LANES), lambda b: (b, 0)),
                   pl.BlockSpec((None, N_EXPERTS, cap), lambda b: (b, 0, 0)),
                   pl.BlockSpec((None, nb, LANES), lambda b: (b, 0, 0))),
        compiler_params=_params("parallel"),
        name="topk",
    )(hext, aff.reshape(batch, packed_rows, LANES))


GATHER_ROWS = 128


def _gather_kernel(idx_ref, h_hbm, xe_ref, g_ref, buf, sem, *, t_len, d):
    b, e, c = pl.program_id(0), pl.program_id(1), pl.program_id(2)
    rc = GATHER_ROWS

    def issue(r, carry):
        row = b * t_len + idx_ref[b * N_EXPERTS + e, c * rc + r]
        pltpu.make_async_copy(h_hbm.at[pl.ds(row, 1)], buf.at[pl.ds(r, 1)], sem).start()
        return carry

    lax.fori_loop(0, rc, issue, 0, unroll=8)
    pltpu.make_async_copy(h_hbm.at[pl.ds(0, rc)], buf, sem).wait()
    xe_ref[...] = buf[:, :d].astype(xe_ref.dtype)
    aff = buf[:, d:]
    lane = lax.broadcasted_iota(jnp.int32, aff.shape, 1)
    g_ref[...] = jnp.broadcast_to(jnp.sum(jnp.where(lane == e, aff, 0.0), axis=1, keepdims=True), aff.shape)


def _gather(idx, hext, batch, t_len, d, cap):
    rc = GATHER_ROWS
    nchunk = cap // rc
    return pl.pallas_call(
        functools.partial(_gather_kernel, t_len=t_len, d=d),
        out_shape=(jax.ShapeDtypeStruct((N_EXPERTS, batch * cap, d), BF16),
                   jax.ShapeDtypeStruct((N_EXPERTS, batch * cap, LANES), F32)),
        grid_spec=pltpu.PrefetchScalarGridSpec(
            num_scalar_prefetch=1,
            grid=(batch, N_EXPERTS, nchunk),
            in_specs=[pl.BlockSpec(memory_space=pl.ANY)],
            out_specs=(pl.BlockSpec((None, rc, d), lambda b, e, c, idx: (e, b * nchunk + c, 0)),
                       pl.BlockSpec((None, rc, LANES), lambda b, e, c, idx: (e, b * nchunk + c, 0))),
            scratch_shapes=[pltpu.VMEM((rc, d + LANES), F32), pltpu.SemaphoreType.DMA(())]),
        compiler_params=_params("arbitrary", "arbitrary", "arbitrary"),
        name="gather",
    )(idx.reshape(batch * N_EXPERTS, cap), hext)


def _expert_up_kernel(x_ref, wg_ref, wu_ref, o_ref):
    x = x_ref[...]
    a = _dot(x, wg_ref[...].astype(BF16))
    u = _dot(x, wu_ref[...].astype(BF16))
    o_ref[...] = (_silu(a) * u).astype(o_ref.dtype)


def _expert_up(xe, w_gate, w_up):
    e, m, d = xe.shape
    ff = w_gate.shape[2]
    tf = 256
    return pl.pallas_call(
        _expert_up_kernel,
        out_shape=jax.ShapeDtypeStruct((e, m, ff), BF16),
        grid=(e, ff // tf),
        in_specs=[pl.BlockSpec((None, m, d), lambda i, f: (i, 0, 0)),
                  pl.BlockSpec((None, d, tf), lambda i, f: (i, 0, f)),
                  pl.BlockSpec((None, d, tf), lambda i, f: (i, 0, f))],
        out_specs=pl.BlockSpec((None, m, tf), lambda i, f: (i, 0, f)),
        compiler_params=_params("parallel", "arbitrary"),
        name="expert_up",
    )(xe, w_gate, w_up)


def _expert_down_kernel(a_ref, w_ref, g_ref, o_ref):
    o_ref[...] = (_dot(a_ref[...], w_ref[...].astype(BF16)) * g_ref[:, :1]).astype(o_ref.dtype)


def _expert_down(act, w_down, g):
    e, m, ff = act.shape
    d = w_down.shape[2]
    tn = 512
    return pl.pallas_call(
        _expert_down_kernel,
        out_shape=jax.ShapeDtypeStruct((e, m, d), BF16),
        grid=(e, d // tn),
        in_specs=[pl.BlockSpec((None, m, ff), lambda i, j: (i, 0, 0)),
                  pl.BlockSpec((None, ff, tn), lambda i, j: (i, 0, j)),
                  pl.BlockSpec((None, m, LANES), lambda i, j: (i, 0, 0))],
        out_specs=pl.BlockSpec((None, m, tn), lambda i, j: (i, 0, j)),
        compiler_params=_params("parallel", "arbitrary"),
        name="expert_down",
    )(act, w_down, g)


COMBINE_WINDOW = 64
ROW_ALIGN = 16


def _combine_kernel(bounds_ref, slot_ref, ye_hbm, x_ref, g_ref, fn_ref, o_ref, stage, onehot, sem, *, nt, cap):
    step = pl.program_id(0)
    w = COMBINE_WINDOW
    cur = step % 2

    def tile_rows(st, e):
        b, i = st // nt, st % nt
        return b, bounds_ref[b * (nt + 1) + i, e], bounds_ref[b * (nt + 1) + i + 1, e]

    def window(st, e, r):
        b, first, _ = tile_rows(st, e)
        base = (first // ROW_ALIGN) * ROW_ALIGN + r * w
        return b, base, jnp.minimum(base, cap - w)

    def window_copy(st, e, r, buf):
        b, _, src = window(st, e, r)
        return pltpu.make_async_copy(ye_hbm.at[e, pl.ds(pl.multiple_of(b * cap + src, ROW_ALIGN), w), :],
                                     stage.at[buf, pl.ds(e * w, w), :], sem.at[buf])

    def start_round(st, r, buf):
        for e in range(N_EXPERTS):
            window_copy(st, e, r, buf).start()

    def wait_round(st, r, buf):
        for e in range(N_EXPERTS):
            window_copy(st, e, r, buf).wait()

    def scatter(r):
        slots = slot_ref[...]
        pos = lax.broadcasted_iota(jnp.int32, (1, w), 1)
        for e in range(N_EXPERTS):
            _, base, src = window(step, e, r)
            col = slots[:, e:e + 1]
            hit = (col >= base) & (col - src == pos)
            onehot[:, e * w:(e + 1) * w] = jnp.where(hit, 1.0, 0.0).astype(BF16)
        return _dot(onehot[...], stage[cur])

    @pl.when(step == 0)
    def _():
        start_round(0, 0, 0)

    @pl.when(step + 1 < pl.num_programs(0))
    def _():
        start_round(step + 1, 0, 1 - cur)

    wait_round(step, 0, cur)
    o_ref[...] = scatter(0)

    rounds = 1
    for e in range(N_EXPERTS):
        _, first, last = tile_rows(step, e)
        rounds = jnp.maximum(rounds, (last - (first // ROW_ALIGN) * ROW_ALIGN + w - 1) // w)

    def extra_round(r, carry):
        start_round(step, r, cur)
        wait_round(step, r, cur)
        o_ref[...] += scatter(r)
        return carry

    lax.fori_loop(1, rounds, extra_round, 0)

    v = x_ref[...] + g_ref[...] * o_ref[...]
    y = v * lax.rsqrt(jnp.mean(v * v, axis=-1, keepdims=True) + NORM_EPS)
    o_ref[...] = y * fn_ref[...]


def _combine(bounds, slot, ye, x2d, mod4, final_norm, batch, t_len, cap, k_gate):
    m, d = x2d.shape
    tm = TOPK_TILE
    nt = t_len // tm
    w = COMBINE_WINDOW
    return pl.pallas_call(
        functools.partial(_combine_kernel, nt=nt, cap=cap),
        out_shape=jax.ShapeDtypeStruct((m, d), F32),
        grid_spec=pltpu.PrefetchScalarGridSpec(
            num_scalar_prefetch=1,
            grid=(batch * nt,),
            in_specs=[pl.BlockSpec((tm, LANES), lambda s, bnd: (s, 0)),
                      pl.BlockSpec(memory_space=pl.ANY),
                      pl.BlockSpec((tm, d), lambda s, bnd: (s, 0)),
                      pl.BlockSpec((None, None, 1, d), lambda s, bnd: (s // nt, k_gate, 0, 0)),
                      pl.BlockSpec((1, d), lambda s, bnd: (0, 0))],
            out_specs=pl.BlockSpec((tm, d), lambda s, bnd: (s, 0)),
            scratch_shapes=[pltpu.VMEM((2, N_EXPERTS * w, d), BF16), pltpu.VMEM((tm, N_EXPERTS * w), BF16),
                            pltpu.SemaphoreType.DMA((2,))]),
        compiler_params=_params("arbitrary"),
        name="combine",
    )(bounds.reshape(batch * (nt + 1), LANES), slot, ye, x2d, mod4, final_norm.reshape(1, d))


def kernel(x, c, ctx, c_ctx, norm1, norm2, w_mod, b_mod, w_in, na_rpb, ret_decay, w_branch_na, w_branch_ret,
           w_out, w_router, w_gate, w_up, w_down, final_norm):
    batch, t_len, d = x.shape
    tc_len = ctx.shape[1]
    na_w = NA_HEADS * NA_HEAD_DIM
    ret_w = RET_HEADS * RET_DIM
    col_qa, col_ka, col_va = 0, na_w, 2 * na_w
    col_qr = 3 * na_w
    col_kr, col_vr, col_gf, col_gb = col_qr + ret_w, col_qr + 2 * ret_w, col_qr + 3 * ret_w, col_qr + 4 * ret_w
    col_ga = col_qr + 5 * ret_w
    col_gb2 = col_ga + d
    cap = EC_CAPACITY_FACTOR * t_len // N_EXPERTS
    assert w_in.shape[0] == 1, "single layer"

    x2d = x.reshape(batch * t_len, d)
    cvec = jnp.concatenate([c, c_ctx[None], jnp.zeros((8 - batch - 1, d), F32)], axis=0)
    mod = _modulation(cvec, w_mod[0], b_mod[0])
    mod4 = mod[:batch + 1].reshape(batch + 1, N_MOD, 1, d)

    h = _prenorm(x2d, norm1[0], mod4, t_len, 0, 0)
    hc = _prenorm(ctx.reshape(batch * tc_len, d), norm1[0], mod4, batch * tc_len, batch, 0)
    tn = 512
    proj = _matmul(h, w_in[0], w_in.shape[2], 2048, tn, lambda j: j, "in_proj")
    kv_w = 2 * na_w
    cproj = _matmul(hc, w_in[0], kv_w + 2 * ret_w, batch * tc_len, tn,
                    lambda j: jnp.where(j < kv_w // tn, col_ka // tn + j, col_kr // tn + j - kv_w // tn), "ctx_proj")

    y_na = _na_attention(proj, cproj, na_rpb[0], batch, t_len, tc_len, col_qa, col_ka, col_va)
    y_ret = _retention(proj, cproj, ret_decay[0], batch, t_len, tc_len, col_qr, col_kr, col_vr, col_gf, col_gb,
                       kv_w, kv_w + ret_w)
    mixed = _merge(y_na, y_ret, w_branch_na[0], w_branch_ret[0], proj, col_ga, col_gb2)
    x1 = _outproj(mixed, w_out[0], x2d, mod4, t_len, 2)

    w_router_pad = jnp.pad(w_router[0], ((0, 0), (0, LANES - N_EXPERTS))).astype(BF16)
    hext, aff = _router(x1, norm2[0], mod4, w_router_pad, t_len, 3)
    slot, idx, bounds = _topk(hext, aff, batch, t_len, d, cap)
    xe, g = _gather(idx, hext, batch, t_len, d, cap)
    act = _expert_up(xe, w_gate[0], w_up[0])
    ye = _expert_down(act, w_down[0], g)
    out = _combine(bounds, slot, ye, x1, mod4, final_norm, batch, t_len, cap, 5)
    return out.reshape(batch, t_len, d)
```

```python
import functools

import numpy as np
import jax
import jax.numpy as jnp
from jax import lax
from jax.experimental import pallas as pl
from jax.experimental.pallas import tpu as pltpu

F32 = jnp.float32
BF16 = jnp.bfloat16

GRID_W = 64
NA_HEADS = 16
NA_HEAD_DIM = 128
NA_WIN_R = 8
NA_WIN_C = 16
RET_HEADS = 8
RET_DIM = 256
RET_CHUNK = 256
N_EXPERTS = 16
EC_CAPACITY_FACTOR = 2
ROPE_BASE = 10000.0
NORM_EPS = 1e-6
NEG_INF = -1e30
N_MOD = 6

VMEM_LIMIT_BYTES = 56 * 1024 * 1024
LANES = 128

NA_Q_ROWS = 4
NA_K_ROWS = NA_Q_ROWS + NA_WIN_R


def _params(*sem):
    return pltpu.CompilerParams(dimension_semantics=sem, vmem_limit_bytes=VMEM_LIMIT_BYTES)


def _dot(a, b):
    return jnp.dot(a, b, preferred_element_type=F32)


def _dot_nt(a, b):
    return lax.dot_general(a, b, (((1,), (1,)), ((), ())), preferred_element_type=F32)


def _dot_tn(a, b):
    return lax.dot_general(a, b, (((0,), (0,)), ((), ())), preferred_element_type=F32)


def _silu(x):
    return x * jax.nn.sigmoid(x)


def _mod_kernel(c_ref, w_ref, b_ref, o_ref):
    a = _silu(c_ref[...]).astype(BF16)
    o_ref[...] = _dot(a, w_ref[...].astype(BF16)) + b_ref[...]


def _modulation(cvec, w_mod, b_mod):
    r, d = cvec.shape
    n = w_mod.shape[1]
    tn = 512
    return pl.pallas_call(
        _mod_kernel,
        out_shape=jax.ShapeDtypeStruct((r, n), F32),
        grid=(n // tn,),
        in_specs=[pl.BlockSpec((r, d), lambda j: (0, 0)),
                  pl.BlockSpec((d, tn), lambda j: (0, j)),
                  pl.BlockSpec((1, tn), lambda j: (0, j))],
        out_specs=pl.BlockSpec((r, tn), lambda j: (0, j)),
        compiler_params=_params("arbitrary"),
        name="modulation",
    )(cvec, w_mod, b_mod.reshape(1, n))


def _modulated_norm(x, g, shift, scale):
    y = x * lax.rsqrt(jnp.mean(x * x, axis=-1, keepdims=True) + NORM_EPS)
    return (y * g) * (1.0 + scale) + shift


def _prenorm_kernel(x_ref, g_ref, sh_ref, sc_ref, o_ref):
    o_ref[...] = _modulated_norm(x_ref[...], g_ref[...], sh_ref[...], sc_ref[...]).astype(o_ref.dtype)


def _prenorm(x2d, gain, mod4, rows_per_sample, sample0, k_shift):
    r, d = x2d.shape
    tr = 256
    per = rows_per_sample // tr
    return pl.pallas_call(
        _prenorm_kernel,
        out_shape=jax.ShapeDtypeStruct((r, d), BF16),
        grid=(r // tr,),
        in_specs=[pl.BlockSpec((tr, d), lambda i: (i, 0)),
                  pl.BlockSpec((1, d), lambda i: (0, 0)),
                  pl.BlockSpec((None, None, 1, d), lambda i: (sample0 + i // per, k_shift, 0, 0)),
                  pl.BlockSpec((None, None, 1, d), lambda i: (sample0 + i // per, k_shift + 1, 0, 0))],
        out_specs=pl.BlockSpec((tr, d), lambda i: (i, 0)),
        compiler_params=_params("parallel"),
        name="prenorm",
    )(x2d, gain.reshape(1, d), mod4, mod4)


def _mm_kernel(a_ref, w_ref, o_ref):
    o_ref[...] = _dot(a_ref[...], w_ref[...].astype(BF16)).astype(o_ref.dtype)


def _matmul(a, w, n_out, tm, tn, col_block, name):
    m, k = a.shape
    return pl.pallas_call(
        _mm_kernel,
        out_shape=jax.ShapeDtypeStruct((m, n_out), BF16),
        grid=(m // tm, n_out // tn),
        in_specs=[pl.BlockSpec((tm, k), lambda i, j: (i, 0), pipeline_mode=pl.Buffered(1)),
                  pl.BlockSpec((k, tn), lambda i, j: (0, col_block(j)))],
        out_specs=pl.BlockSpec((tm, tn), lambda i, j: (i, j)),
        compiler_params=_params("parallel", "arbitrary"),
        name=name,
    )(a, w)


def _na_tables(rows):
    wr = NA_WIN_R
    bases, tables = [], []
    for t in range(rows // NA_Q_ROWS):
        kb = int(np.clip(NA_Q_ROWS * t - wr // 2, 0, rows - NA_K_ROWS))
        tab = []
        for i in range(NA_Q_ROWS):
            r = NA_Q_ROWS * t + i
            r0 = int(np.clip(r - wr // 2, 0, rows - wr))
            tab.append(tuple((kb + j - r + NA_WIN_R - 1) if r0 <= kb + j < r0 + wr else None
                             for j in range(NA_K_ROWS)))
        bases.append(kb)
        tables.append(tuple(tab))
    uniq = list(dict.fromkeys(tables))
    return bases, uniq, [uniq.index(t) for t in tables]


def _na_bias(rpb, patterns):
    h = rpb.shape[0]
    cidx = np.arange(GRID_W)
    c0 = np.clip(cidx - NA_WIN_C // 2, 0, GRID_W - NA_WIN_C)
    col_ok = (cidx[None, :] >= c0[:, None]) & (cidx[None, :] < c0[:, None] + NA_WIN_C)
    dc = np.clip(cidx[None, :] - cidx[:, None], -(NA_WIN_C - 1), NA_WIN_C - 1) + (NA_WIN_C - 1)
    tile = jnp.where(col_ok[None, None], rpb.astype(F32)[:, :, dc], NEG_INF)
    neg = jnp.full((h, GRID_W, GRID_W), NEG_INF, F32)
    pats = []
    for tab in patterns:
        pats.append(jnp.concatenate(
            [jnp.concatenate([neg if dr is None else tile[:, dr] for dr in row], axis=2) for row in tab], axis=1))
    return jnp.stack(pats)


def _na_kernel(q_ref, k_ref, v_ref, kc_ref, vc_ref, bias_ref, o_ref, *, rows, pat_ids):
    tq = NA_Q_ROWS * GRID_W
    nk = NA_K_ROWS * GRID_W
    scale = NA_HEAD_DIM ** -0.5

    def tile(t, carry):
        kb = pl.multiple_of(jnp.clip(NA_Q_ROWS * t - NA_WIN_R // 2, 0, rows - NA_K_ROWS) * GRID_W, tq)
        q0 = pl.multiple_of(t * tq, tq)
        pat = 0
        for i, pid in enumerate(pat_ids):
            pat = jnp.where(t == i, pid, pat)
        q = q_ref[pl.ds(q0, tq), :]
        s_w = _dot_nt(q, k_ref[pl.ds(kb, nk), :]) * scale + bias_ref[pat]
        s_c = _dot_nt(q, kc_ref[...]) * scale
        m = jnp.maximum(jnp.max(s_w, axis=-1, keepdims=True), jnp.max(s_c, axis=-1, keepdims=True))
        p_w = jnp.exp(s_w - m)
        p_c = jnp.exp(s_c - m)
        l = jnp.sum(p_w, axis=-1, keepdims=True) + jnp.sum(p_c, axis=-1, keepdims=True)
        o = _dot(p_w.astype(BF16), v_ref[pl.ds(kb, nk), :]) + _dot(p_c.astype(BF16), vc_ref[...])
        o_ref[pl.ds(q0, tq), :] = (o / l).astype(o_ref.dtype)
        return carry

    lax.fori_loop(0, rows // NA_Q_ROWS, tile, 0, unroll=2)


def _na_attention(proj, cproj, rpb, batch, t_len, tc_len, col_q, col_k, col_v):
    rows = t_len // GRID_W
    tq = NA_Q_ROWS * GRID_W
    _, patterns, pat_ids = _na_tables(rows)
    bias = _na_bias(rpb, patterns)
    dh = NA_HEAD_DIM
    tok = lambda col: pl.BlockSpec((t_len, dh), lambda b, h: (b, col // dh + h))
    return pl.pallas_call(
        functools.partial(_na_kernel, rows=rows, pat_ids=tuple(pat_ids)),
        out_shape=jax.ShapeDtypeStruct((batch * t_len, NA_HEADS * dh), BF16),
        grid=(batch, NA_HEADS),
        in_specs=[tok(col_q), tok(col_k), tok(col_v),
                  pl.BlockSpec((tc_len, dh), lambda b, h: (b, h)),
                  pl.BlockSpec((tc_len, dh), lambda b, h: (b, NA_HEADS + h)),
                  pl.BlockSpec((len(patterns), None, tq, NA_K_ROWS * GRID_W), lambda b, h: (0, h, 0, 0))],
        out_specs=pl.BlockSpec((t_len, dh), lambda b, h: (b, h)),
        compiler_params=_params("parallel", "arbitrary"),
        name="na_attention",
    )(proj, proj, proj, cproj, cproj, bias)


def _rope_tables(t_len):
    quarter = RET_DIM // 4
    inv = ROPE_BASE ** (-np.arange(quarter, dtype=np.float64) / quarter)
    tpos = np.arange(t_len)
    row_ang = (tpos // GRID_W).astype(np.float64)[:, None] * inv[None, :]
    col_ang = (tpos % GRID_W).astype(np.float64)[:, None] * inv[None, :]
    cos = np.concatenate([np.cos(row_ang)] * 2 + [np.cos(col_ang)] * 2, axis=-1)
    sin = np.concatenate([-np.sin(row_ang), np.sin(row_ang), -np.sin(col_ang), np.sin(col_ang)], axis=-1)
    return jnp.asarray(cos, F32), jnp.asarray(sin, F32)


def _rope(a, cos, sin):
    half = RET_DIM // 2
    swapped = jnp.concatenate([pltpu.roll(a[:, :half], half // 2, 1), pltpu.roll(a[:, half:], half // 2, 1)], axis=1)
    return a * cos + swapped * sin


def _ret_kernel(dec_ref, q_ref, k_ref, v_ref, gf_ref, gb_ref, kc_ref, vc_ref, cos_ref, sin_ref, o_ref,
                sf_ref, sb_ref, acc_ref):
    h = pl.program_id(1)
    c = RET_CHUNK
    t_len = q_ref.shape[0]
    tc_len = kc_ref.shape[0]
    nc = t_len // c
    k_scale = RET_DIM ** -0.5

    def log_gamma(direction):
        e = jnp.full((1, 1), dec_ref[direction, h], F32)
        return jnp.log1p(-jnp.exp2(-e))

    lg_f, lg_b = log_gamma(0), log_gamma(1)
    pos = lax.broadcasted_iota(jnp.int32, (c, 1), 0).astype(F32)
    diff = pos - lax.broadcasted_iota(jnp.int32, (1, c), 1).astype(F32)
    dec_f = jnp.where(diff >= 0, jnp.exp(lg_f * jnp.maximum(diff, 0.0)), 0.0)
    dec_b = jnp.where(diff <= 0, jnp.exp(lg_b * jnp.maximum(-diff, 0.0)), 0.0)
    qdec_f, kdec_f, cdec_f = jnp.exp(lg_f * (pos + 1.0)), jnp.exp(lg_f * (c - 1.0 - pos)), jnp.exp(lg_f * c)
    qdec_b, kdec_b, cdec_b = jnp.exp(lg_b * (c - pos)), jnp.exp(lg_b * pos), jnp.exp(lg_b * c)

    cpos = lax.broadcasted_iota(jnp.int32, (tc_len, 1), 0).astype(F32)
    kc = kc_ref[...].astype(F32) * k_scale
    vc = vc_ref[...]
    sf_ref[...] = _dot_tn((kc * jnp.exp(lg_f * (tc_len - 1.0 - cpos))).astype(BF16), vc)
    sb_ref[...] = _dot_tn((kc * jnp.exp(lg_b * cpos)).astype(BF16), vc)
    acc_ref[...] = jnp.zeros_like(acc_ref)

    def chunk(n, s_ref, dec, qdec, kdec, cdec, g_ref):
        r0 = pl.multiple_of(n * c, c)
        cos = cos_ref[pl.ds(r0, c), :]
        sin = sin_ref[pl.ds(r0, c), :]
        q = _rope(q_ref[pl.ds(r0, c), :].astype(F32), cos, sin)
        k = _rope(k_ref[pl.ds(r0, c), :].astype(F32), cos, sin) * k_scale
        v = v_ref[pl.ds(r0, c), :]
        qb = q.astype(BF16)
        scores = _dot_nt(qb, k.astype(BF16)) * dec
        s = s_ref[...]
        o = _dot(scores.astype(BF16), v) + _dot(qb, s.astype(BF16)) * qdec
        s_ref[...] = s * cdec + _dot_tn((k * kdec).astype(BF16), v)
        on = o * lax.rsqrt(jnp.mean(o * o, axis=-1, keepdims=True) + NORM_EPS)
        acc_ref[pl.ds(r0, c), :] += _silu(g_ref[pl.ds(r0, c), :].astype(F32)) * on

    def body(n, carry):
        chunk(n, sf_ref, dec_f, qdec_f, kdec_f, cdec_f, gf_ref)
        chunk(nc - 1 - n, sb_ref, dec_b, qdec_b, kdec_b, cdec_b, gb_ref)
        return carry

    lax.fori_loop(0, nc, body, 0)
    o_ref[...] = acc_ref[...].astype(o_ref.dtype)


def _retention(proj, cproj, ret_decay, batch, t_len, tc_len, col_q, col_k, col_v, col_gf, col_gb, ccol_k, ccol_v):
    d = RET_DIM
    cos, sin = _rope_tables(t_len)
    tok = lambda col: pl.BlockSpec((t_len, d), lambda b, h: (b, col // d + h))
    ctx = lambda col: pl.BlockSpec((tc_len, d), lambda b, h: (b, col // d + h))
    tab = pl.BlockSpec((t_len, d), lambda b, h: (0, 0))
    return pl.pallas_call(
        _ret_kernel,
        out_shape=jax.ShapeDtypeStruct((batch * t_len, RET_HEADS * d), BF16),
        grid=(batch, RET_HEADS),
        in_specs=[pl.BlockSpec(memory_space=pltpu.SMEM),
                  tok(col_q), tok(col_k), tok(col_v), tok(col_gf), tok(col_gb), ctx(ccol_k), ctx(ccol_v), tab, tab],
        out_specs=pl.BlockSpec((t_len, d), lambda b, h: (b, h)),
        scratch_shapes=[pltpu.VMEM((d, d), F32), pltpu.VMEM((d, d), F32), pltpu.VMEM((t_len, d), F32)],
        compiler_params=_params("parallel", "arbitrary"),
        name="retention",
    )(ret_decay.astype(F32), proj, proj, proj, proj, proj, cproj, cproj, cos, sin)


def _merge_kernel(ya_ref, yr_ref, wa_ref, wr_ref, ga_ref, gb_ref, o_ref):
    a = _dot(ya_ref[...], wa_ref[...].astype(BF16))
    r = _dot(yr_ref[...], wr_ref[...].astype(BF16))
    o_ref[...] = (jax.nn.sigmoid(ga_ref[...].astype(F32)) * a + jax.nn.sigmoid(gb_ref[...].astype(F32)) * r
                  ).astype(o_ref.dtype)


def _merge(y_na, y_ret, w_na, w_ret, proj, col_ga, col_gb):
    m, ka = y_na.shape
    kr = y_ret.shape[1]
    n = w_na.shape[1]
    tm, tn = 1024, 512
    return pl.pallas_call(
        _merge_kernel,
        out_shape=jax.ShapeDtypeStruct((m, n), BF16),
        grid=(m // tm, n // tn),
        in_specs=[pl.BlockSpec((tm, ka), lambda i, j: (i, 0)),
                  pl.BlockSpec((tm, kr), lambda i, j: (i, 0)),
                  pl.BlockSpec((ka, tn), lambda i, j: (0, j)),
                  pl.BlockSpec((kr, tn), lambda i, j: (0, j)),
                  pl.BlockSpec((tm, tn), lambda i, j: (i, col_ga // tn + j)),
                  pl.BlockSpec((tm, tn), lambda i, j: (i, col_gb // tn + j))],
        out_specs=pl.BlockSpec((tm, tn), lambda i, j: (i, j)),
        compiler_params=_params("parallel", "arbitrary"),
        name="merge",
    )(y_na, y_ret, w_na, w_ret, proj, proj)


def _outproj_kernel(m_ref, w_ref, x_ref, g_ref, o_ref):
    o_ref[...] = x_ref[...] + g_ref[...] * _dot(m_ref[...], w_ref[...].astype(BF16))


def _outproj(mixed, w_out, x2d, mod4, rows_per_sample, k_gate):
    m, k = mixed.shape
    n = w_out.shape[1]
    tm, tn = 1024, 512
    per = rows_per_sample // tm
    return pl.pallas_call(
        _outproj_kernel,
        out_shape=jax.ShapeDtypeStruct((m, n), F32),
        grid=(m // tm, n // tn),
        in_specs=[pl.BlockSpec((tm, k), lambda i, j: (i, 0)),
                  pl.BlockSpec((k, tn), lambda i, j: (0, j)),
                  pl.BlockSpec((tm, tn), lambda i, j: (i, j)),
                  pl.BlockSpec((None, None, 1, tn), lambda i, j: (i // per, k_gate, 0, j))],
        out_specs=pl.BlockSpec((tm, tn), lambda i, j: (i, j)),
        compiler_params=_params("parallel", "arbitrary"),
        name="outproj",
    )(mixed, w_out, x2d, mod4)


def _router_kernel(x_ref, g_ref, sh_ref, sc_ref, wr_ref, o_ref, a_ref):
    d = x_ref.shape[1]
    h = _modulated_norm(x_ref[...], g_ref[...], sh_ref[...], sc_ref[...])
    logits = _dot(h.astype(BF16), wr_ref[...])
    lane = lax.broadcasted_iota(jnp.int32, logits.shape, 1)
    logits = jnp.where(lane < N_EXPERTS, logits, NEG_INF)
    p = jnp.exp(logits - jnp.max(logits, axis=-1, keepdims=True))
    aff = p / jnp.sum(p, axis=-1, keepdims=True)
    o_ref[:, :d] = h
    o_ref[:, d:] = aff
    a_ref[...] = aff[:, :N_EXPERTS]


def _router(x2d, gain, mod4, w_router_pad, rows_per_sample, k_shift):
    r, d = x2d.shape
    tr = 256
    per = rows_per_sample // tr
    return pl.pallas_call(
        _router_kernel,
        out_shape=(jax.ShapeDtypeStruct((r, d + LANES), F32), jax.ShapeDtypeStruct((r, N_EXPERTS), F32)),
        grid=(r // tr,),
        in_specs=[pl.BlockSpec((tr, d), lambda i: (i, 0)),
                  pl.BlockSpec((1, d), lambda i: (0, 0)),
                  pl.BlockSpec((None, None, 1, d), lambda i: (i // per, k_shift, 0, 0)),
                  pl.BlockSpec((None, None, 1, d), lambda i: (i // per, k_shift + 1, 0, 0)),
                  pl.BlockSpec((d, LANES), lambda i: (0, 0))],
        out_specs=(pl.BlockSpec((tr, d + LANES), lambda i: (i, 0)), pl.BlockSpec((tr, N_EXPERTS), lambda i: (i, 0))),
        compiler_params=_params("parallel"),
        name="router",
    )(x2d, gain.reshape(1, d), mod4, mod4, w_router_pad)


TOPK_TILE = 256
GEOMETRIC_STEPS = 32
ARITHMETIC_STEPS = 12


def _topk_kernel(aff_ref, affc_ref, slot_ref, idx_ref, bounds_ref, *, cap):
    t_len = aff_ref.shape[0]
    tt = TOPK_TILE
    nt = t_len // tt
    packed = affc_ref[...]

    def per_expert(v):
        shift = LANES // 2
        while shift >= N_EXPERTS:
            v = v + pltpu.roll(v, shift, 1)
            shift //= 2
        return v

    def narrow(c, mid):
        lo, hi = c
        ge = per_expert(jnp.sum(jnp.where(packed >= mid, 1.0, 0.0), axis=0, keepdims=True)) >= cap
        return jnp.where(ge, mid, lo), jnp.where(ge, hi, mid)

    tiny = float(np.finfo(np.float32).tiny)
    above_tiny = per_expert(jnp.sum(jnp.where(packed >= tiny, 1.0, 0.0), axis=0, keepdims=True)) >= cap
    bracket = (jnp.where(above_tiny, tiny, 0.0), jnp.where(above_tiny, 2.0, tiny) + jnp.zeros((1, LANES), F32))
    geometric_mid = lambda c: jnp.clip(jnp.sqrt(c[0]) * jnp.sqrt(c[1]), c[0], c[1])
    bracket = lax.fori_loop(0, GEOMETRIC_STEPS, lambda _, c: narrow(c, geometric_mid(c)), bracket)
    lo, hi = lax.fori_loop(0, ARITHMETIC_STEPS, lambda _, c: narrow(c, 0.5 * (c[0] + c[1])), bracket)

    def count_above(i, cnt):
        r0 = pl.multiple_of(i * tt, tt)
        return cnt + jnp.sum(jnp.where(aff_ref[pl.ds(r0, tt), :] >= hi, 1.0, 0.0), axis=0, keepdims=True)
    need = cap - lax.fori_loop(0, nt, count_above, jnp.zeros((1, LANES), F32))

    tri = jnp.where(lax.broadcasted_iota(jnp.int32, (tt, tt), 0) >= lax.broadcasted_iota(jnp.int32, (tt, tt), 1),
                    1.0, 0.0).astype(BF16)

    def assign(i, carry):
        eq_before, sel_before = carry
        r0 = pl.multiple_of(i * tt, tt)
        a = aff_ref[pl.ds(r0, tt), :]
        above = a >= hi
        tie = (a >= lo) & (a < hi)
        eq = jnp.where(tie, 1.0, 0.0)
        eq_rank = _dot(tri, eq.astype(BF16)) + eq_before
        sel = jnp.where(above | (tie & (eq_rank <= need)), 1.0, 0.0)
        sel_rank = _dot(tri, sel.astype(BF16)) + sel_before
        slot_ref[pl.ds(r0, tt), :] = jnp.where(sel > 0, sel_rank - 1.0, -1.0).astype(jnp.int32)
        bounds_ref[pl.ds(i, 1), :] = sel_before.astype(jnp.int32)
        return (eq_before + jnp.sum(eq, axis=0, keepdims=True), sel_before + jnp.sum(sel, axis=0, keepdims=True))

    zero = jnp.zeros((1, LANES), F32)
    _, total = lax.fori_loop(0, nt, assign, (zero, zero))
    bounds_ref[nt:nt + 1, :] = total.astype(jnp.int32)

    sub = lax.broadcasted_iota(jnp.int32, (8, tt), 0)
    lane_tok = lax.broadcasted_iota(jnp.int32, (8, tt), 1)
    slot_iota = lax.broadcasted_iota(jnp.int32, (1, cap), 1)
    for e in range(N_EXPERTS):
        def body(i, acc):
            r0 = pl.multiple_of(i * tt, tt)
            tok = lane_tok + r0
            parts = jnp.where(sub == 0, tok >> 6, jnp.where(sub == 1, tok & 63, 0)).astype(F32).astype(BF16)
            onehot = jnp.where(slot_ref[pl.ds(r0, tt), e:e + 1] == slot_iota, 1.0, 0.0).astype(BF16)
            return acc + _dot(parts, onehot)
        acc = lax.fori_loop(0, nt, body, jnp.zeros((8, cap), F32))
        idx_ref[e:e + 1, :] = (acc[0:1] * 64.0 + acc[1:2]).astype(jnp.int32)


def _topk(hext, aff, batch, t_len, d, cap):
    nb = t_len // TOPK_TILE + 1
    packed_rows = t_len * N_EXPERTS // LANES
    return pl.pallas_call(
        functools.partial(_topk_kernel, cap=cap),
        out_shape=(jax.ShapeDtypeStruct((batch * t_len, LANES), jnp.int32),
                   jax.ShapeDtypeStruct((batch, N_EXPERTS, cap), jnp.int32),
                   jax.ShapeDtypeStruct((batch, nb, LANES), jnp.int32)),
        grid=(batch,),
        in_specs=[pl.BlockSpec((t_len, LANES), lambda b: (b, d // LANES)),
                  pl.BlockSpec((None, packed_rows, LANES), lambda b: (b, 0, 0))],
        out_specs=(pl.BlockSpec((t_len, LANES), lambda b: (b, 0)),
                   pl.BlockSpec((None, N_EXPERTS, cap), lambda b: (b, 0, 0)),
                   pl.BlockSpec((None, nb, LANES), lambda b: (b, 0, 0))),
        compiler_params=_params("parallel"),
        name="topk",
    )(hext, aff.reshape(batch, packed_rows, LANES))


GATHER_ROWS = 128


def _gather_kernel(idx_ref, h_hbm, xe_ref, g_ref, buf, sem, *, t_len, d, nchunk):
    step = pl.program_id(0)
    rc = GATHER_ROWS
    cur = step % 2

    def request(st, slot):
        lst, chunk = st // nchunk, st % nchunk
        row0 = (lst // N_EXPERTS) * t_len

        def issue(r, carry):
            row = row0 + idx_ref[lst, chunk * rc + r]
            pltpu.make_async_copy(h_hbm.at[pl.ds(row, 1)], buf.at[slot, pl.ds(r, 1)], sem.at[slot]).start()
            return carry

        lax.fori_loop(0, rc, issue, 0, unroll=8)

    @pl.when(step == 0)
    def _():
        request(0, 0)

    @pl.when(step + 1 < pl.num_programs(0))
    def _():
        request(step + 1, 1 - cur)

    pltpu.make_async_copy(h_hbm.at[pl.ds(0, rc)], buf.at[cur], sem.at[cur]).wait()
    rows = buf[cur]
    xe_ref[...] = rows[:, :d].astype(xe_ref.dtype)
    aff = rows[:, d:]
    e = (step // nchunk) % N_EXPERTS
    lane = lax.broadcasted_iota(jnp.int32, aff.shape, 1)
    g_ref[...] = jnp.broadcast_to(jnp.sum(jnp.where(lane == e, aff, 0.0), axis=1, keepdims=True), aff.shape)


def _gather(idx, hext, batch, t_len, d, cap):
    rc = GATHER_ROWS
    nchunk = cap // rc

    def out_block(s, idx):
        lst = s // nchunk
        return lst % N_EXPERTS, (lst // N_EXPERTS) * nchunk + s % nchunk, 0

    return pl.pallas_call(
        functools.partial(_gather_kernel, t_len=t_len, d=d, nchunk=nchunk),
        out_shape=(jax.ShapeDtypeStruct((N_EXPERTS, batch * cap, d), BF16),
                   jax.ShapeDtypeStruct((N_EXPERTS, batch * cap, LANES), F32)),
        grid_spec=pltpu.PrefetchScalarGridSpec(
            num_scalar_prefetch=1,
            grid=(batch * N_EXPERTS * nchunk,),
            in_specs=[pl.BlockSpec(memory_space=pl.ANY)],
            out_specs=(pl.BlockSpec((None, rc, d), out_block), pl.BlockSpec((None, rc, LANES), out_block)),
            scratch_shapes=[pltpu.VMEM((2, rc, d + LANES), F32), pltpu.SemaphoreType.DMA((2,))]),
        compiler_params=_params("arbitrary"),
        name="gather",
    )(idx.reshape(batch * N_EXPERTS, cap), hext)


def _expert_up_kernel(x_ref, wg_ref, wu_ref, o_ref):
    x = x_ref[...]
    a = _dot(x, wg_ref[...].astype(BF16))
    u = _dot(x, wu_ref[...].astype(BF16))
    o_ref[...] = (_silu(a) * u).astype(o_ref.dtype)


def _expert_up(xe, w_gate, w_up):
    e, m, d = xe.shape
    ff = w_gate.shape[2]
    tf = 256
    return pl.pallas_call(
        _expert_up_kernel,
        out_shape=jax.ShapeDtypeStruct((e, m, ff), BF16),
        grid=(e, ff // tf),
        in_specs=[pl.BlockSpec((None, m, d), lambda i, f: (i, 0, 0)),
                  pl.BlockSpec((None, d, tf), lambda i, f: (i, 0, f)),
                  pl.BlockSpec((None, d, tf), lambda i, f: (i, 0, f))],
        out_specs=pl.BlockSpec((None, m, tf), lambda i, f: (i, 0, f)),
        compiler_params=_params("parallel", "arbitrary"),
        name="expert_up",
    )(xe, w_gate, w_up)


def _expert_down_kernel(a_ref, w_ref, g_ref, o_ref):
    o_ref[...] = (_dot(a_ref[...], w_ref[...].astype(BF16)) * g_ref[:, :1]).astype(o_ref.dtype)


def _expert_down(act, w_down, g):
    e, m, ff = act.shape
    d = w_down.shape[2]
    tn = min(1024, d)
    return pl.pallas_call(
        _expert_down_kernel,
        out_shape=jax.ShapeDtypeStruct((e, m, d), BF16),
        grid=(e, d // tn),
        in_specs=[pl.BlockSpec((None, m, ff), lambda i, j: (i, 0, 0)),
                  pl.BlockSpec((None, ff, tn), lambda i, j: (i, 0, j)),
                  pl.BlockSpec((None, m, LANES), lambda i, j: (i, 0, 0))],
        out_specs=pl.BlockSpec((None, m, tn), lambda i, j: (i, 0, j)),
        compiler_params=_params("parallel", "arbitrary"),
        name="expert_down",
    )(act, w_down, g)


COMBINE_WINDOW = 64
ROW_ALIGN = 16


def _combine_kernel(bounds_ref, slot_ref, ye_hbm, x_ref, g_ref, fn_ref, o_ref, stage, onehot, sem, *, nt, cap):
    step = pl.program_id(0)
    w = COMBINE_WINDOW
    cur = step % 2

    def tile_rows(st, e):
        b, i = st // nt, st % nt
        return b, bounds_ref[b * (nt + 1) + i, e], bounds_ref[b * (nt + 1) + i + 1, e]

    def window(st, e, r):
        b, first, _ = tile_rows(st, e)
        base = (first // ROW_ALIGN) * ROW_ALIGN + r * w
        return b, base, jnp.minimum(base, cap - w)

    def window_copy(st, e, r, buf):
        b, _, src = window(st, e, r)
        return pltpu.make_async_copy(ye_hbm.at[e, pl.ds(pl.multiple_of(b * cap + src, ROW_ALIGN), w), :],
                                     stage.at[buf, pl.ds(e * w, w), :], sem.at[buf])

    def start_round(st, r, buf):
        for e in range(N_EXPERTS):
            window_copy(st, e, r, buf).start()

    def wait_round(st, r, buf):
        for e in range(N_EXPERTS):
            window_copy(st, e, r, buf).wait()

    def scatter(r):
        slots = slot_ref[...]
        pos = lax.broadcasted_iota(jnp.int32, (1, w), 1)
        for e in range(N_EXPERTS):
            _, base, src = window(step, e, r)
            col = slots[:, e:e + 1]
            hit = (col >= base) & (col - src == pos)
            onehot[:, e * w:(e + 1) * w] = jnp.where(hit, 1.0, 0.0).astype(BF16)
        return _dot(onehot[...], stage[cur])

    @pl.when(step == 0)
    def _():
        start_round(0, 0, 0)

    @pl.when(step + 1 < pl.num_programs(0))
    def _():
        start_round(step + 1, 0, 1 - cur)

    wait_round(step, 0, cur)
    o_ref[...] = scatter(0)

    rounds = 1
    for e in range(N_EXPERTS):
        _, first, last = tile_rows(step, e)
        rounds = jnp.maximum(rounds, (last - (first // ROW_ALIGN) * ROW_ALIGN + w - 1) // w)

    def extra_round(r, carry):
        start_round(step, r, cur)
        wait_round(step, r, cur)
        o_ref[...] += scatter(r)
        return carry

    lax.fori_loop(1, rounds, extra_round, 0)

    v = x_ref[...] + g_ref[...] * o_ref[...]
    y = v * lax.rsqrt(jnp.mean(v * v, axis=-1, keepdims=True) + NORM_EPS)
    o_ref[...] = y * fn_ref[...]


def _combine(bounds, slot, ye, x2d, mod4, final_norm, batch, t_len, cap, k_gate):
    m, d = x2d.shape
    tm = TOPK_TILE
    nt = t_len // tm
    w = COMBINE_WINDOW
    return pl.pallas_call(
        functools.partial(_combine_kernel, nt=nt, cap=cap),
        out_shape=jax.ShapeDtypeStruct((m, d), F32),
        grid_spec=pltpu.PrefetchScalarGridSpec(
            num_scalar_prefetch=1,
            grid=(batch * nt,),
            in_specs=[pl.BlockSpec((tm, LANES), lambda s, bnd: (s, 0)),
                      pl.BlockSpec(memory_space=pl.ANY),
                      pl.BlockSpec((tm, d), lambda s, bnd: (s, 0)),
                      pl.BlockSpec((None, None, 1, d), lambda s, bnd: (s // nt, k_gate, 0, 0)),
                      pl.BlockSpec((1, d), lambda s, bnd: (0, 0))],
            out_specs=pl.BlockSpec((tm, d), lambda s, bnd: (s, 0)),
            scratch_shapes=[pltpu.VMEM((2, N_EXPERTS * w, d), BF16), pltpu.VMEM((tm, N_EXPERTS * w), BF16),
                            pltpu.SemaphoreType.DMA((2,))]),
        compiler_params=_params("arbitrary"),
        name="combine",
    )(bounds.reshape(batch * (nt + 1), LANES), slot, ye, x2d, mod4, final_norm.reshape(1, d))


def kernel(x, c, ctx, c_ctx, norm1, norm2, w_mod, b_mod, w_in, na_rpb, ret_decay, w_branch_na, w_branch_ret,
           w_out, w_router, w_gate, w_up, w_down, final_norm):
    batch, t_len, d = x.shape
    tc_len = ctx.shape[1]
    na_w = NA_HEADS * NA_HEAD_DIM
    ret_w = RET_HEADS * RET_DIM
    col_qa, col_ka, col_va = 0, na_w, 2 * na_w
    col_qr = 3 * na_w
    col_kr, col_vr, col_gf, col_gb = col_qr + ret_w, col_qr + 2 * ret_w, col_qr + 3 * ret_w, col_qr + 4 * ret_w
    col_ga = col_qr + 5 * ret_w
    col_gb2 = col_ga + d
    cap = EC_CAPACITY_FACTOR * t_len // N_EXPERTS
    assert w_in.shape[0] == 1, "single layer"

    x2d = x.reshape(batch * t_len, d)
    cvec = jnp.concatenate([c, c_ctx[None], jnp.zeros((8 - batch - 1, d), F32)], axis=0)
    mod = _modulation(cvec, w_mod[0], b_mod[0])
    mod4 = mod[:batch + 1].reshape(batch + 1, N_MOD, 1, d)

    h = _prenorm(x2d, norm1[0], mod4, t_len, 0, 0)
    hc = _prenorm(ctx.reshape(batch * tc_len, d), norm1[0], mod4, batch * tc_len, batch, 0)
    tn = 512
    proj = _matmul(h, w_in[0], w_in.shape[2], 2048, tn, lambda j: j, "in_proj")
    kv_w = 2 * na_w
    cproj = _matmul(hc, w_in[0], kv_w + 2 * ret_w, batch * tc_len, tn,
                    lambda j: jnp.where(j < kv_w // tn, col_ka // tn + j, col_kr // tn + j - kv_w // tn), "ctx_proj")

    y_na = _na_attention(proj, cproj, na_rpb[0], batch, t_len, tc_len, col_qa, col_ka, col_va)
    y_ret = _retention(proj, cproj, ret_decay[0], batch, t_len, tc_len, col_qr, col_kr, col_vr, col_gf, col_gb,
                       kv_w, kv_w + ret_w)
    mixed = _merge(y_na, y_ret, w_branch_na[0], w_branch_ret[0], proj, col_ga, col_gb2)
    x1 = _outproj(mixed, w_out[0], x2d, mod4, t_len, 2)

    w_router_pad = jnp.pad(w_router[0], ((0, 0), (0, LANES - N_EXPERTS))).astype(BF16)
    hext, aff = _router(x1, norm2[0], mod4, w_router_pad, t_len, 3)
    slot, idx, bounds = _topk(hext, aff, batch, t_len, d, cap)
    xe, g = _gather(idx, hext, batch, t_len, d, cap)
    act = _expert_up(xe, w_gate[0], w_up[0])
    ye = _expert_down(act, w_down[0], g)
    out = _combine(bounds, slot, ye, x1, mod4, final_norm, batch, t_len, cap, 5)
    return out.reshape(batch, t_len, d)
```

```python
import functools

import numpy as np
import jax
import jax.numpy as jnp
from jax import lax
from jax.experimental import pallas as pl
from jax.experimental.pallas import tpu as pltpu

F32 = jnp.float32
BF16 = jnp.bfloat16

GRID_W = 64
NA_HEADS = 16
NA_HEAD_DIM = 128
NA_WIN_R = 8
NA_WIN_C = 16
RET_HEADS = 8
RET_DIM = 256
RET_CHUNK = 256
N_EXPERTS = 16
EC_CAPACITY_FACTOR = 2
ROPE_BASE = 10000.0
NORM_EPS = 1e-6
NEG_INF = -1e30
N_MOD = 6

VMEM_LIMIT_BYTES = 56 * 1024 * 1024
LANES = 128

NA_Q_ROWS = 4
NA_K_ROWS = NA_Q_ROWS + NA_WIN_R


def _params(*sem):
    return pltpu.CompilerParams(dimension_semantics=sem, vmem_limit_bytes=VMEM_LIMIT_BYTES)


def _dot(a, b):
    return jnp.dot(a, b, preferred_element_type=F32)


def _dot_nt(a, b):
    return lax.dot_general(a, b, (((1,), (1,)), ((), ())), preferred_element_type=F32)


def _dot_tn(a, b):
    return lax.dot_general(a, b, (((0,), (0,)), ((), ())), preferred_element_type=F32)


def _silu(x):
    return x * jax.nn.sigmoid(x)


def _mod_kernel(c_ref, w_ref, b_ref, o_ref):
    a = _silu(c_ref[...]).astype(BF16)
    o_ref[...] = _dot(a, w_ref[...].astype(BF16)) + b_ref[...]


def _modulation(cvec, w_mod, b_mod):
    r, d = cvec.shape
    n = w_mod.shape[1]
    tn = 512
    return pl.pallas_call(
        _mod_kernel,
        out_shape=jax.ShapeDtypeStruct((r, n), F32),
        grid=(n // tn,),
        in_specs=[pl.BlockSpec((r, d), lambda j: (0, 0)),
                  pl.BlockSpec((d, tn), lambda j: (0, j)),
                  pl.BlockSpec((1, tn), lambda j: (0, j))],
        out_specs=pl.BlockSpec((r, tn), lambda j: (0, j)),
        compiler_params=_params("arbitrary"),
        name="modulation",
    )(cvec, w_mod, b_mod.reshape(1, n))


def _modulated_norm(x, g, shift, scale):
    y = x * lax.rsqrt(jnp.mean(x * x, axis=-1, keepdims=True) + NORM_EPS)
    return (y * g) * (1.0 + scale) + shift


def _prenorm_kernel(x_ref, g_ref, sh_ref, sc_ref, o_ref):
    o_ref[...] = _modulated_norm(x_ref[...], g_ref[...], sh_ref[...], sc_ref[...]).astype(o_ref.dtype)


def _prenorm(x2d, gain, mod4, rows_per_sample, sample0, k_shift):
    r, d = x2d.shape
    tr = 512
    per = rows_per_sample // tr
    return pl.pallas_call(
        _prenorm_kernel,
        out_shape=jax.ShapeDtypeStruct((r, d), BF16),
        grid=(r // tr,),
        in_specs=[pl.BlockSpec((tr, d), lambda i: (i, 0)),
                  pl.BlockSpec((1, d), lambda i: (0, 0)),
                  pl.BlockSpec((None, None, 1, d), lambda i: (sample0 + i // per, k_shift, 0, 0)),
                  pl.BlockSpec((None, None, 1, d), lambda i: (sample0 + i // per, k_shift + 1, 0, 0))],
        out_specs=pl.BlockSpec((tr, d), lambda i: (i, 0)),
        compiler_params=_params("parallel"),
        name="prenorm",
    )(x2d, gain.reshape(1, d), mod4, mod4)


def _mm_kernel(a_ref, w_ref, o_ref):
    o_ref[...] = _dot(a_ref[...], w_ref[...].astype(BF16)).astype(o_ref.dtype)


def _matmul(a, w, n_out, tm, tn, col_block, name):
    m, k = a.shape
    return pl.pallas_call(
        _mm_kernel,
        out_shape=jax.ShapeDtypeStruct((m, n_out), BF16),
        grid=(m // tm, n_out // tn),
        in_specs=[pl.BlockSpec((tm, k), lambda i, j: (i, 0), pipeline_mode=pl.Buffered(1)),
                  pl.BlockSpec((k, tn), lambda i, j: (0, col_block(j)))],
        out_specs=pl.BlockSpec((tm, tn), lambda i, j: (i, j)),
        compiler_params=_params("parallel", "arbitrary"),
        name=name,
    )(a, w)


def _na_tables(rows):
    wr = NA_WIN_R
    bases, tables = [], []
    for t in range(rows // NA_Q_ROWS):
        kb = int(np.clip(NA_Q_ROWS * t - wr // 2, 0, rows - NA_K_ROWS))
        tab = []
        for i in range(NA_Q_ROWS):
            r = NA_Q_ROWS * t + i
            r0 = int(np.clip(r - wr // 2, 0, rows - wr))
            tab.append(tuple((kb + j - r + NA_WIN_R - 1) if r0 <= kb + j < r0 + wr else None
                             for j in range(NA_K_ROWS)))
        bases.append(kb)
        tables.append(tuple(tab))
    uniq = list(dict.fromkeys(tables))
    return bases, uniq, [uniq.index(t) for t in tables]


def _na_bias_tiles(rpb):
    cidx = np.arange(GRID_W)
    c0 = np.clip(cidx - NA_WIN_C // 2, 0, GRID_W - NA_WIN_C)
    col_ok = (cidx[None, :] >= c0[:, None]) & (cidx[None, :] < c0[:, None] + NA_WIN_C)
    dc = np.clip(cidx[None, :] - cidx[:, None], -(NA_WIN_C - 1), NA_WIN_C - 1) + (NA_WIN_C - 1)
    return jnp.where(col_ok[None, None], rpb.astype(F32)[:, :, dc], NEG_INF)


def _na_kernel(q_ref, k_ref, v_ref, kc_ref, vc_ref, tile_ref, o_ref, bias_ref, *, rows, patterns, pat_ids):
    tq = NA_Q_ROWS * GRID_W
    nk = NA_K_ROWS * GRID_W
    scale = NA_HEAD_DIM ** -0.5

    outside = jnp.full((GRID_W, GRID_W), NEG_INF, F32)
    block = lambda dr: outside if dr is None else tile_ref[dr]
    for p, tab in enumerate(patterns):
        for i, row in enumerate(tab):
            for j in range(0, NA_K_ROWS, 2):
                bias_ref[p, i * GRID_W:(i + 1) * GRID_W, j * GRID_W:(j + 2) * GRID_W] = jnp.concatenate(
                    [block(row[j]), block(row[j + 1])], axis=1)

    def tile(t, carry):
        kb = pl.multiple_of(jnp.clip(NA_Q_ROWS * t - NA_WIN_R // 2, 0, rows - NA_K_ROWS) * GRID_W, tq)
        q0 = pl.multiple_of(t * tq, tq)
        pat = 0
        for i, pid in enumerate(pat_ids):
            pat = jnp.where(t == i, pid, pat)
        q = q_ref[pl.ds(q0, tq), :]
        s_w = _dot_nt(q, k_ref[pl.ds(kb, nk), :]) * scale + bias_ref[pat]
        s_c = _dot_nt(q, kc_ref[...]) * scale
        m = jnp.maximum(jnp.max(s_w, axis=-1, keepdims=True), jnp.max(s_c, axis=-1, keepdims=True))
        p_w = jnp.exp(s_w - m)
        p_c = jnp.exp(s_c - m)
        l = jnp.sum(p_w, axis=-1, keepdims=True) + jnp.sum(p_c, axis=-1, keepdims=True)
        o = _dot(p_w.astype(BF16), v_ref[pl.ds(kb, nk), :]) + _dot(p_c.astype(BF16), vc_ref[...])
        o_ref[pl.ds(q0, tq), :] = (o / l).astype(o_ref.dtype)
        return carry

    lax.fori_loop(0, rows // NA_Q_ROWS, tile, 0, unroll=2)


def _na_attention(proj, cproj, rpb, batch, t_len, tc_len, col_q, col_k, col_v):
    rows = t_len // GRID_W
    tq = NA_Q_ROWS * GRID_W
    _, patterns, pat_ids = _na_tables(rows)
    tiles = _na_bias_tiles(rpb)
    dh = NA_HEAD_DIM
    tok = lambda col: pl.BlockSpec((t_len, dh), lambda b, h: (b, col // dh + h))
    return pl.pallas_call(
        functools.partial(_na_kernel, rows=rows, patterns=tuple(patterns), pat_ids=tuple(pat_ids)),
        out_shape=jax.ShapeDtypeStruct((batch * t_len, NA_HEADS * dh), BF16),
        grid=(batch, NA_HEADS),
        in_specs=[tok(col_q), tok(col_k), tok(col_v),
                  pl.BlockSpec((tc_len, dh), lambda b, h: (b, h)),
                  pl.BlockSpec((tc_len, dh), lambda b, h: (b, NA_HEADS + h)),
                  pl.BlockSpec((None,) + tiles.shape[1:], lambda b, h: (h, 0, 0, 0))],
        out_specs=pl.BlockSpec((t_len, dh), lambda b, h: (b, h)),
        scratch_shapes=[pltpu.VMEM((len(patterns), tq, NA_K_ROWS * GRID_W), F32)],
        compiler_params=_params("parallel", "arbitrary"),
        name="na_attention",
    )(proj, proj, proj, cproj, cproj, tiles)


def _rope_tables(t_len):
    quarter = RET_DIM // 4
    inv = ROPE_BASE ** (-np.arange(quarter, dtype=np.float64) / quarter)
    tpos = np.arange(t_len)
    row_ang = (tpos // GRID_W).astype(np.float64)[:, None] * inv[None, :]
    col_ang = (tpos % GRID_W).astype(np.float64)[:, None] * inv[None, :]
    cos = np.concatenate([np.cos(row_ang)] * 2 + [np.cos(col_ang)] * 2, axis=-1)
    sin = np.concatenate([-np.sin(row_ang), np.sin(row_ang), -np.sin(col_ang), np.sin(col_ang)], axis=-1)
    return jnp.asarray(cos, F32), jnp.asarray(sin, F32)


def _rope(a, cos, sin):
    half = RET_DIM // 2
    swapped = jnp.concatenate([pltpu.roll(a[:, :half], half // 2, 1), pltpu.roll(a[:, half:], half // 2, 1)], axis=1)
    return a * cos + swapped * sin


def _ret_kernel(dec_ref, q_ref, k_ref, v_ref, gf_ref, gb_ref, kc_ref, vc_ref, cos_ref, sin_ref, o_ref,
                sf_ref, sb_ref, acc_ref):
    h = pl.program_id(1)
    c = RET_CHUNK
    t_len = q_ref.shape[0]
    tc_len = kc_ref.shape[0]
    nc = t_len // c
    k_scale = RET_DIM ** -0.5

    def log_gamma(direction):
        e = jnp.full((1, 1), dec_ref[direction, h], F32)
        return jnp.log1p(-jnp.exp2(-e))

    lg_f, lg_b = log_gamma(0), log_gamma(1)
    pos = lax.broadcasted_iota(jnp.int32, (c, 1), 0).astype(F32)
    diff = pos - lax.broadcasted_iota(jnp.int32, (1, c), 1).astype(F32)
    dec_f = jnp.where(diff >= 0, jnp.exp(lg_f * jnp.maximum(diff, 0.0)), 0.0) * k_scale
    dec_b = jnp.where(diff <= 0, jnp.exp(lg_b * jnp.maximum(-diff, 0.0)), 0.0) * k_scale
    qdec_f, kdec_f, cdec_f = jnp.exp(lg_f * (pos + 1.0)), jnp.exp(lg_f * (c - 1.0 - pos)) * k_scale, jnp.exp(lg_f * c)
    qdec_b, kdec_b, cdec_b = jnp.exp(lg_b * (c - pos)), jnp.exp(lg_b * pos) * k_scale, jnp.exp(lg_b * c)

    cpos = lax.broadcasted_iota(jnp.int32, (tc_len, 1), 0).astype(F32)
    kc = kc_ref[...].astype(F32) * k_scale
    vc = vc_ref[...]
    sf_ref[...] = _dot_tn((kc * jnp.exp(lg_f * (tc_len - 1.0 - cpos))).astype(BF16), vc)
    sb_ref[...] = _dot_tn((kc * jnp.exp(lg_b * cpos)).astype(BF16), vc)
    acc_ref[...] = jnp.zeros_like(acc_ref)

    def chunk(n, s_ref, dec, qdec, kdec, cdec, g_ref):
        r0 = pl.multiple_of(n * c, c)
        cos = cos_ref[pl.ds(r0, c), :]
        sin = sin_ref[pl.ds(r0, c), :]
        q = _rope(q_ref[pl.ds(r0, c), :].astype(F32), cos, sin)
        k = _rope(k_ref[pl.ds(r0, c), :].astype(F32), cos, sin)
        v = v_ref[pl.ds(r0, c), :]
        qb = q.astype(BF16)
        scores = _dot_nt(qb, k.astype(BF16)) * dec
        s = s_ref[...]
        o = _dot(scores.astype(BF16), v) + _dot(qb, s.astype(BF16)) * qdec
        s_ref[...] = s * cdec + _dot_tn((k * kdec).astype(BF16), v)
        on = o * lax.rsqrt(jnp.mean(o * o, axis=-1, keepdims=True) + NORM_EPS)
        acc_ref[pl.ds(r0, c), :] += _silu(g_ref[pl.ds(r0, c), :].astype(F32)) * on

    def body(n, carry):
        chunk(n, sf_ref, dec_f, qdec_f, kdec_f, cdec_f, gf_ref)
        chunk(nc - 1 - n, sb_ref, dec_b, qdec_b, kdec_b, cdec_b, gb_ref)
        return carry

    lax.fori_loop(0, nc, body, 0)
    o_ref[...] = acc_ref[...].astype(o_ref.dtype)


def _retention(proj, cproj, ret_decay, batch, t_len, tc_len, col_q, col_k, col_v, col_gf, col_gb, ccol_k, ccol_v):
    d = RET_DIM
    cos, sin = _rope_tables(t_len)
    tok = lambda col: pl.BlockSpec((t_len, d), lambda b, h: (b, col // d + h))
    ctx = lambda col: pl.BlockSpec((tc_len, d), lambda b, h: (b, col // d + h))
    tab = pl.BlockSpec((t_len, d), lambda b, h: (0, 0))
    return pl.pallas_call(
        _ret_kernel,
        out_shape=jax.ShapeDtypeStruct((batch * t_len, RET_HEADS * d), BF16),
        grid=(batch, RET_HEADS),
        in_specs=[pl.BlockSpec(memory_space=pltpu.SMEM),
                  tok(col_q), tok(col_k), tok(col_v), tok(col_gf), tok(col_gb), ctx(ccol_k), ctx(ccol_v), tab, tab],
        out_specs=pl.BlockSpec((t_len, d), lambda b, h: (b, h)),
        scratch_shapes=[pltpu.VMEM((d, d), F32), pltpu.VMEM((d, d), F32), pltpu.VMEM((t_len, d), F32)],
        compiler_params=_params("parallel", "arbitrary"),
        name="retention",
    )(ret_decay.astype(F32), proj, proj, proj, proj, proj, cproj, cproj, cos, sin)


def _merge_kernel(ya_ref, yr_ref, wa_ref, wr_ref, ga_ref, gb_ref, o_ref):
    a = _dot(ya_ref[...], wa_ref[...].astype(BF16))
    r = _dot(yr_ref[...], wr_ref[...].astype(BF16))
    o_ref[...] = (jax.nn.sigmoid(ga_ref[...].astype(F32)) * a + jax.nn.sigmoid(gb_ref[...].astype(F32)) * r
                  ).astype(o_ref.dtype)


def _merge(y_na, y_ret, w_na, w_ret, proj, col_ga, col_gb):
    m, ka = y_na.shape
    kr = y_ret.shape[1]
    n = w_na.shape[1]
    tm, tn = 1024, 512
    return pl.pallas_call(
        _merge_kernel,
        out_shape=jax.ShapeDtypeStruct((m, n), BF16),
        grid=(m // tm, n // tn),
        in_specs=[pl.BlockSpec((tm, ka), lambda i, j: (i, 0)),
                  pl.BlockSpec((tm, kr), lambda i, j: (i, 0)),
                  pl.BlockSpec((ka, tn), lambda i, j: (0, j)),
                  pl.BlockSpec((kr, tn), lambda i, j: (0, j)),
                  pl.BlockSpec((tm, tn), lambda i, j: (i, col_ga // tn + j)),
                  pl.BlockSpec((tm, tn), lambda i, j: (i, col_gb // tn + j))],
        out_specs=pl.BlockSpec((tm, tn), lambda i, j: (i, j)),
        compiler_params=_params("parallel", "arbitrary"),
        name="merge",
    )(y_na, y_ret, w_na, w_ret, proj, proj)


def _outproj_kernel(m_ref, w_ref, x_ref, g_ref, o_ref):
    o_ref[...] = x_ref[...] + g_ref[...] * _dot(m_ref[...], w_ref[...].astype(BF16))


def _outproj(mixed, w_out, x2d, mod4, rows_per_sample, k_gate):
    m, k = mixed.shape
    n = w_out.shape[1]
    tm, tn = 1024, 512
    per = rows_per_sample // tm
    return pl.pallas_call(
        _outproj_kernel,
        out_shape=jax.ShapeDtypeStruct((m, n), F32),
        grid=(m // tm, n // tn),
        in_specs=[pl.BlockSpec((tm, k), lambda i, j: (i, 0)),
                  pl.BlockSpec((k, tn), lambda i, j: (0, j)),
                  pl.BlockSpec((tm, tn), lambda i, j: (i, j)),
                  pl.BlockSpec((None, None, 1, tn), lambda i, j: (i // per, k_gate, 0, j))],
        out_specs=pl.BlockSpec((tm, tn), lambda i, j: (i, j)),
        compiler_params=_params("parallel", "arbitrary"),
        name="outproj",
    )(mixed, w_out, x2d, mod4)


def _router_kernel(x_ref, g_ref, sh_ref, sc_ref, wr_ref, o_ref, a_ref):
    d = x_ref.shape[1]
    h = _modulated_norm(x_ref[...], g_ref[...], sh_ref[...], sc_ref[...])
    logits = _dot(h.astype(BF16), wr_ref[...])
    lane = lax.broadcasted_iota(jnp.int32, logits.shape, 1)
    logits = jnp.where(lane < N_EXPERTS, logits, NEG_INF)
    p = jnp.exp(logits - jnp.max(logits, axis=-1, keepdims=True))
    aff = p / jnp.sum(p, axis=-1, keepdims=True)
    o_ref[:, :d] = h
    o_ref[:, d:] = aff
    a_ref[...] = aff[:, :N_EXPERTS]


def _router(x2d, gain, mod4, w_router_pad, rows_per_sample, k_shift):
    r, d = x2d.shape
    tr = 512
    per = rows_per_sample // tr
    return pl.pallas_call(
        _router_kernel,
        out_shape=(jax.ShapeDtypeStruct((r, d + LANES), F32), jax.ShapeDtypeStruct((r, N_EXPERTS), F32)),
        grid=(r // tr,),
        in_specs=[pl.BlockSpec((tr, d), lambda i: (i, 0)),
                  pl.BlockSpec((1, d), lambda i: (0, 0)),
                  pl.BlockSpec((None, None, 1, d), lambda i: (i // per, k_shift, 0, 0)),
                  pl.BlockSpec((None, None, 1, d), lambda i: (i // per, k_shift + 1, 0, 0)),
                  pl.BlockSpec((d, LANES), lambda i: (0, 0))],
        out_specs=(pl.BlockSpec((tr, d + LANES), lambda i: (i, 0)), pl.BlockSpec((tr, N_EXPERTS), lambda i: (i, 0))),
        compiler_params=_params("parallel"),
        name="router",
    )(x2d, gain.reshape(1, d), mod4, mod4, w_router_pad)


TOPK_TILE = 256
GEOMETRIC_STEPS = 32
ARITHMETIC_STEPS = 12


def _topk_kernel(aff_ref, affc_ref, slot_ref, idx_ref, bounds_ref, slot_t_ref, *, cap):
    t_len = aff_ref.shape[0]
    tt = TOPK_TILE
    nt = t_len // tt
    packed = affc_ref[...]

    def per_expert(v):
        shift = LANES // 2
        while shift >= N_EXPERTS:
            v = v + pltpu.roll(v, shift, 1)
            shift //= 2
        return v

    def narrow(c, mid):
        lo, hi = c
        ge = per_expert(jnp.sum(jnp.where(packed >= mid, 1.0, 0.0), axis=0, keepdims=True)) >= cap
        return jnp.where(ge, mid, lo), jnp.where(ge, hi, mid)

    tiny = float(np.finfo(np.float32).tiny)
    above_tiny = per_expert(jnp.sum(jnp.where(packed >= tiny, 1.0, 0.0), axis=0, keepdims=True)) >= cap
    bracket = (jnp.where(above_tiny, tiny, 0.0), jnp.where(above_tiny, 2.0, tiny) + jnp.zeros((1, LANES), F32))
    geometric_mid = lambda c: jnp.clip(jnp.sqrt(c[0]) * jnp.sqrt(c[1]), c[0], c[1])
    bracket = lax.fori_loop(0, GEOMETRIC_STEPS, lambda _, c: narrow(c, geometric_mid(c)), bracket)
    lo, hi = lax.fori_loop(0, ARITHMETIC_STEPS, lambda _, c: narrow(c, 0.5 * (c[0] + c[1])), bracket)

    def count_above(i, cnt):
        r0 = pl.multiple_of(i * tt, tt)
        return cnt + jnp.sum(jnp.where(aff_ref[pl.ds(r0, tt), :] >= hi, 1.0, 0.0), axis=0, keepdims=True)
    need = cap - lax.fori_loop(0, nt, count_above, jnp.zeros((1, LANES), F32))

    tri = jnp.where(lax.broadcasted_iota(jnp.int32, (tt, tt), 0) >= lax.broadcasted_iota(jnp.int32, (tt, tt), 1),
                    1.0, 0.0).astype(BF16)

    def assign(i, carry):
        eq_before, sel_before = carry
        r0 = pl.multiple_of(i * tt, tt)
        a = aff_ref[pl.ds(r0, tt), :]
        above = a >= hi
        tie = (a >= lo) & (a < hi)
        eq = jnp.where(tie, 1.0, 0.0)
        eq_rank = _dot(tri, eq.astype(BF16)) + eq_before
        sel = jnp.where(above | (tie & (eq_rank <= need)), 1.0, 0.0)
        sel_rank = _dot(tri, sel.astype(BF16)) + sel_before
        slot_ref[pl.ds(r0, tt), :] = jnp.where(sel > 0, sel_rank - 1.0, -1.0).astype(jnp.int32)
        bounds_ref[pl.ds(i, 1), :] = sel_before.astype(jnp.int32)
        return (eq_before + jnp.sum(eq, axis=0, keepdims=True), sel_before + jnp.sum(sel, axis=0, keepdims=True))

    zero = jnp.zeros((1, LANES), F32)
    _, total = lax.fori_loop(0, nt, assign, (zero, zero))
    bounds_ref[nt:nt + 1, :] = total.astype(jnp.int32)

    slot_t_ref[...] = jnp.transpose(slot_ref[...].astype(F32))
    idx_ref[...] = jnp.zeros_like(idx_ref)
    sublanes = 8
    tok = lax.broadcasted_iota(jnp.int32, (sublanes, t_len), 1).astype(F32)
    sub = lax.broadcasted_iota(jnp.int32, (sublanes, 1), 0).astype(F32)
    for e in range(N_EXPERTS):
        def body(g, carry):
            s0 = pl.multiple_of(g * sublanes, sublanes)
            hit = slot_t_ref[e:e + 1, :] == sub + s0.astype(F32)
            idx_ref[pl.ds(s0, sublanes), e:e + 1] = jnp.sum(jnp.where(hit, tok, 0.0), axis=1,
                                                            keepdims=True).astype(jnp.int32)
            return carry
        lax.fori_loop(0, cap // sublanes, body, 0, unroll=8)


def _topk(hext, aff, batch, t_len, d, cap):
    nb = t_len // TOPK_TILE + 1
    packed_rows = t_len * N_EXPERTS // LANES
    slot, idx_t, bounds = pl.pallas_call(
        functools.partial(_topk_kernel, cap=cap),
        out_shape=(jax.ShapeDtypeStruct((batch * t_len, LANES), jnp.int32),
                   jax.ShapeDtypeStruct((batch, cap, LANES), jnp.int32),
                   jax.ShapeDtypeStruct((batch, nb, LANES), jnp.int32)),
        grid=(batch,),
        in_specs=[pl.BlockSpec((t_len, LANES), lambda b: (b, d // LANES)),
                  pl.BlockSpec((None, packed_rows, LANES), lambda b: (b, 0, 0))],
        out_specs=(pl.BlockSpec((t_len, LANES), lambda b: (b, 0)),
                   pl.BlockSpec((None, cap, LANES), lambda b: (b, 0, 0)),
                   pl.BlockSpec((None, nb, LANES), lambda b: (b, 0, 0))),
        scratch_shapes=[pltpu.VMEM((LANES, t_len), F32)],
        compiler_params=_params("parallel"),
        name="topk",
    )(hext, aff.reshape(batch, packed_rows, LANES))
    return slot, idx_t[:, :, :N_EXPERTS].transpose(0, 2, 1), bounds


GATHER_ROWS = 128


def _gather_kernel(idx_ref, h_hbm, xe_ref, g_ref, buf, sem, *, t_len, d, nchunk):
    step = pl.program_id(0)
    rc = GATHER_ROWS
    cur = step % 2

    def request(st, slot):
        lst, chunk = st // nchunk, st % nchunk
        row0 = (lst // N_EXPERTS) * t_len

        def issue(r, carry):
            row = row0 + idx_ref[lst, chunk * rc + r]
            pltpu.make_async_copy(h_hbm.at[pl.ds(row, 1)], buf.at[slot, pl.ds(r, 1)], sem.at[slot]).start()
            return carry

        lax.fori_loop(0, rc, issue, 0, unroll=8)

    @pl.when(step == 0)
    def _():
        request(0, 0)

    @pl.when(step + 1 < pl.num_programs(0))
    def _():
        request(step + 1, 1 - cur)

    pltpu.make_async_copy(h_hbm.at[pl.ds(0, rc)], buf.at[cur], sem.at[cur]).wait()
    rows = buf[cur]
    xe_ref[...] = rows[:, :d].astype(xe_ref.dtype)
    aff = rows[:, d:]
    e = (step // nchunk) % N_EXPERTS
    lane = lax.broadcasted_iota(jnp.int32, aff.shape, 1)
    g_ref[...] = jnp.broadcast_to(jnp.sum(jnp.where(lane == e, aff, 0.0), axis=1, keepdims=True), aff.shape)


def _gather(idx, hext, batch, t_len, d, cap):
    rc = GATHER_ROWS
    nchunk = cap // rc

    def out_block(s, idx):
        lst = s // nchunk
        return lst % N_EXPERTS, (lst // N_EXPERTS) * nchunk + s % nchunk, 0

    return pl.pallas_call(
        functools.partial(_gather_kernel, t_len=t_len, d=d, nchunk=nchunk),
        out_shape=(jax.ShapeDtypeStruct((N_EXPERTS, batch * cap, d), BF16),
                   jax.ShapeDtypeStruct((N_EXPERTS, batch * cap, LANES), F32)),
        grid_spec=pltpu.PrefetchScalarGridSpec(
            num_scalar_prefetch=1,
            grid=(batch * N_EXPERTS * nchunk,),
            in_specs=[pl.BlockSpec(memory_space=pl.ANY)],
            out_specs=(pl.BlockSpec((None, rc, d), out_block), pl.BlockSpec((None, rc, LANES), out_block)),
            scratch_shapes=[pltpu.VMEM((2, rc, d + LANES), F32), pltpu.SemaphoreType.DMA((2,))]),
        compiler_params=_params("arbitrary"),
        name="gather",
    )(idx.reshape(batch * N_EXPERTS, cap), hext)


def _expert_up_kernel(x_ref, wg_ref, wu_ref, o_ref):
    x = x_ref[...]
    a = _dot(x, wg_ref[...].astype(BF16))
    u = _dot(x, wu_ref[...].astype(BF16))
    o_ref[...] = (_silu(a) * u).astype(o_ref.dtype)


def _expert_up(xe, w_gate, w_up):
    e, m, d = xe.shape
    ff = w_gate.shape[2]
    tf = 256
    return pl.pallas_call(
        _expert_up_kernel,
        out_shape=jax.ShapeDtypeStruct((e, m, ff), BF16),
        grid=(e, ff // tf),
        in_specs=[pl.BlockSpec((None, m, d), lambda i, f: (i, 0, 0)),
                  pl.BlockSpec((None, d, tf), lambda i, f: (i, 0, f)),
                  pl.BlockSpec((None, d, tf), lambda i, f: (i, 0, f))],
        out_specs=pl.BlockSpec((None, m, tf), lambda i, f: (i, 0, f)),
        compiler_params=_params("parallel", "arbitrary"),
        name="expert_up",
    )(xe, w_gate, w_up)


def _expert_down_kernel(a_ref, w_ref, g_ref, o_ref):
    o_ref[...] = (_dot(a_ref[...], w_ref[...].astype(BF16)) * g_ref[:, :1]).astype(o_ref.dtype)


def _expert_down(act, w_down, g):
    e, m, ff = act.shape
    d = w_down.shape[2]
    tn = min(1024, d)
    return pl.pallas_call(
        _expert_down_kernel,
        out_shape=jax.ShapeDtypeStruct((e, m, d), BF16),
        grid=(e, d // tn),
        in_specs=[pl.BlockSpec((None, m, ff), lambda i, j: (i, 0, 0)),
                  pl.BlockSpec((None, ff, tn), lambda i, j: (i, 0, j)),
                  pl.BlockSpec((None, m, LANES), lambda i, j: (i, 0, 0))],
        out_specs=pl.BlockSpec((None, m, tn), lambda i, j: (i, 0, j)),
        compiler_params=_params("parallel", "arbitrary"),
        name="expert_down",
    )(act, w_down, g)


COMBINE_WINDOW = 64
ROW_ALIGN = 16


def _combine_kernel(bounds_ref, slot_ref, ye_hbm, x_ref, g_ref, fn_ref, o_ref, stage, onehot, sem, *, nt, cap):
    step = pl.program_id(0)
    w = COMBINE_WINDOW
    cur = step % 2

    def tile_rows(st, e):
        b, i = st // nt, st % nt
        return b, bounds_ref[b * (nt + 1) + i, e], bounds_ref[b * (nt + 1) + i + 1, e]

    def window(st, e, r):
        b, first, _ = tile_rows(st, e)
        base = (first // ROW_ALIGN) * ROW_ALIGN + r * w
        return b, base, jnp.minimum(base, cap - w)

    def window_copy(st, e, r, buf):
        b, _, src = window(st, e, r)
        return pltpu.make_async_copy(ye_hbm.at[e, pl.ds(pl.multiple_of(b * cap + src, ROW_ALIGN), w), :],
                                     stage.at[buf, pl.ds(e * w, w), :], sem.at[buf])

    def start_round(st, r, buf):
        for e in range(N_EXPERTS):
            window_copy(st, e, r, buf).start()

    def wait_round(st, r, buf):
        for e in range(N_EXPERTS):
            window_copy(st, e, r, buf).wait()

    def scatter(r):
        slots = slot_ref[...]
        pos = lax.broadcasted_iota(jnp.int32, (1, w), 1)
        for e in range(N_EXPERTS):
            _, base, src = window(step, e, r)
            col = slots[:, e:e + 1]
            hit = (col >= base) & (col - src == pos)
            onehot[:, e * w:(e + 1) * w] = jnp.where(hit, 1.0, 0.0).astype(BF16)
        return _dot(onehot[...], stage[cur])

    @pl.when(step == 0)
    def _():
        start_round(0, 0, 0)

    @pl.when(step + 1 < pl.num_programs(0))
    def _():
        start_round(step + 1, 0, 1 - cur)

    wait_round(step, 0, cur)
    o_ref[...] = scatter(0)

    rounds = 1
    for e in range(N_EXPERTS):
        _, first, last = tile_rows(step, e)
        rounds = jnp.maximum(rounds, (last - (first // ROW_ALIGN) * ROW_ALIGN + w - 1) // w)

    def extra_round(r, carry):
        start_round(step, r, cur)
        wait_round(step, r, cur)
        o_ref[...] += scatter(r)
        return carry

    lax.fori_loop(1, rounds, extra_round, 0)

    v = x_ref[...] + g_ref[...] * o_ref[...]
    y = v * lax.rsqrt(jnp.mean(v * v, axis=-1, keepdims=True) + NORM_EPS)
    o_ref[...] = y * fn_ref[...]


def _combine(bounds, slot, ye, x2d, mod4, final_norm, batch, t_len, cap, k_gate):
    m, d = x2d.shape
    tm = TOPK_TILE
    nt = t_len // tm
    w = COMBINE_WINDOW
    return pl.pallas_call(
        functools.partial(_combine_kernel, nt=nt, cap=cap),
        out_shape=jax.ShapeDtypeStruct((m, d), F32),
        grid_spec=pltpu.PrefetchScalarGridSpec(
            num_scalar_prefetch=1,
            grid=(batch * nt,),
            in_specs=[pl.BlockSpec((tm, LANES), lambda s, bnd: (s, 0)),
                      pl.BlockSpec(memory_space=pl.ANY),
                      pl.BlockSpec((tm, d), lambda s, bnd: (s, 0)),
                      pl.BlockSpec((None, None, 1, d), lambda s, bnd: (s // nt, k_gate, 0, 0)),
                      pl.BlockSpec((1, d), lambda s, bnd: (0, 0))],
            out_specs=pl.BlockSpec((tm, d), lambda s, bnd: (s, 0)),
            scratch_shapes=[pltpu.VMEM((2, N_EXPERTS * w, d), BF16), pltpu.VMEM((tm, N_EXPERTS * w), BF16),
                            pltpu.SemaphoreType.DMA((2,))]),
        compiler_params=_params("arbitrary"),
        name="combine",
    )(bounds.reshape(batch * (nt + 1), LANES), slot, ye, x2d, mod4, final_norm.reshape(1, d))


def kernel(x, c, ctx, c_ctx, norm1, norm2, w_mod, b_mod, w_in, na_rpb, ret_decay, w_branch_na, w_branch_ret,
           w_out, w_router, w_gate, w_up, w_down, final_norm):
    batch, t_len, d = x.shape
    tc_len = ctx.shape[1]
    na_w = NA_HEADS * NA_HEAD_DIM
    ret_w = RET_HEADS * RET_DIM
    col_qa, col_ka, col_va = 0, na_w, 2 * na_w
    col_qr = 3 * na_w
    col_kr, col_vr, col_gf, col_gb = col_qr + ret_w, col_qr + 2 * ret_w, col_qr + 3 * ret_w, col_qr + 4 * ret_w
    col_ga = col_qr + 5 * ret_w
    col_gb2 = col_ga + d
    cap = EC_CAPACITY_FACTOR * t_len // N_EXPERTS
    assert w_in.shape[0] == 1, "single layer"

    x2d = x.reshape(batch * t_len, d)
    cvec = jnp.concatenate([c, c_ctx[None], jnp.zeros((8 - batch - 1, d), F32)], axis=0)
    mod = _modulation(cvec, w_mod[0], b_mod[0])
    mod4 = mod[:batch + 1].reshape(batch + 1, N_MOD, 1, d)

    h = _prenorm(x2d, norm1[0], mod4, t_len, 0, 0)
    hc = _prenorm(ctx.reshape(batch * tc_len, d), norm1[0], mod4, batch * tc_len, batch, 0)
    tn = 512
    proj = _matmul(h, w_in[0], w_in.shape[2], 2048, tn, lambda j: j, "in_proj")
    kv_w = 2 * na_w
    cproj = _matmul(hc, w_in[0], kv_w + 2 * ret_w, batch * tc_len, tn,
                    lambda j: jnp.where(j < kv_w // tn, col_ka // tn + j, col_kr // tn + j - kv_w // tn), "ctx_proj")

    y_na = _na_attention(proj, cproj, na_rpb[0], batch, t_len, tc_len, col_qa, col_ka, col_va)
    y_ret = _retention(proj, cproj, ret_decay[0], batch, t_len, tc_len, col_qr, col_kr, col_vr, col_gf, col_gb,
                       kv_w, kv_w + ret_w)
    mixed = _merge(y_na, y_ret, w_branch_na[0], w_branch_ret[0], proj, col_ga, col_gb2)
    x1 = _outproj(mixed, w_out[0], x2d, mod4, t_len, 2)

    w_router_pad = jnp.pad(w_router[0], ((0, 0), (0, LANES - N_EXPERTS))).astype(BF16)
    hext, aff = _router(x1, norm2[0], mod4, w_router_pad, t_len, 3)
    slot, idx, bounds = _topk(hext, aff, batch, t_len, d, cap)
    xe, g = _gather(idx, hext, batch, t_len, d, cap)
    act = _expert_up(xe, w_gate[0], w_up[0])
    ye = _expert_down(act, w_down[0], g)
    out = _combine(bounds, slot, ye, x1, mod4, final_norm, batch, t_len, cap, 5)
    return out.reshape(batch, t_len, d)
```

```python
import functools

import numpy as np
import jax
import jax.numpy as jnp
from jax import lax
from jax.experimental import pallas as pl
from jax.experimental.pallas import tpu as pltpu

F32 = jnp.float32
BF16 = jnp.bfloat16

GRID_W = 64
NA_HEADS = 16
NA_HEAD_DIM = 128
NA_WIN_R = 8
NA_WIN_C = 16
RET_HEADS = 8
RET_DIM = 256
RET_CHUNK = 256
N_EXPERTS = 16
EC_CAPACITY_FACTOR = 2
ROPE_BASE = 10000.0
NORM_EPS = 1e-6
NEG_INF = -1e30
N_MOD = 6

VMEM_LIMIT_BYTES = 56 * 1024 * 1024
LANES = 128

NA_Q_ROWS = 4
NA_K_ROWS = NA_Q_ROWS + NA_WIN_R


def _params(*sem):
    return pltpu.CompilerParams(dimension_semantics=sem, vmem_limit_bytes=VMEM_LIMIT_BYTES)


def _dot(a, b):
    return jnp.dot(a, b, preferred_element_type=F32)


def _dot_nt(a, b):
    return lax.dot_general(a, b, (((1,), (1,)), ((), ())), preferred_element_type=F32)


def _dot_tn(a, b):
    return lax.dot_general(a, b, (((0,), (0,)), ((), ())), preferred_element_type=F32)


def _silu(x):
    return x * jax.nn.sigmoid(x)


def _mod_kernel(c_ref, w_ref, b_ref, o_ref):
    a = _silu(c_ref[...]).astype(BF16)
    o_ref[...] = _dot(a, w_ref[...].astype(BF16)) + b_ref[...]


def _modulation(cvec, w_mod, b_mod):
    r, d = cvec.shape
    n = w_mod.shape[1]
    tn = 512
    return pl.pallas_call(
        _mod_kernel,
        out_shape=jax.ShapeDtypeStruct((r, n), F32),
        grid=(n // tn,),
        in_specs=[pl.BlockSpec((r, d), lambda j: (0, 0)),
                  pl.BlockSpec((d, tn), lambda j: (0, j)),
                  pl.BlockSpec((1, tn), lambda j: (0, j))],
        out_specs=pl.BlockSpec((r, tn), lambda j: (0, j)),
        compiler_params=_params("arbitrary"),
        name="modulation",
    )(cvec, w_mod, b_mod.reshape(1, n))


def _modulated_norm(x, g, shift, scale):
    y = x * lax.rsqrt(jnp.mean(x * x, axis=-1, keepdims=True) + NORM_EPS)
    return (y * g) * (1.0 + scale) + shift


def _prenorm_kernel(x_ref, g_ref, sh_ref, sc_ref, o_ref):
    o_ref[...] = _modulated_norm(x_ref[...], g_ref[...], sh_ref[...], sc_ref[...]).astype(o_ref.dtype)


def _prenorm(x2d, gain, mod4, rows_per_sample, sample0, k_shift):
    r, d = x2d.shape
    tr = 512
    per = rows_per_sample // tr
    return pl.pallas_call(
        _prenorm_kernel,
        out_shape=jax.ShapeDtypeStruct((r, d), BF16),
        grid=(r // tr,),
        in_specs=[pl.BlockSpec((tr, d), lambda i: (i, 0)),
                  pl.BlockSpec((1, d), lambda i: (0, 0)),
                  pl.BlockSpec((None, None, 1, d), lambda i: (sample0 + i // per, k_shift, 0, 0)),
                  pl.BlockSpec((None, None, 1, d), lambda i: (sample0 + i // per, k_shift + 1, 0, 0))],
        out_specs=pl.BlockSpec((tr, d), lambda i: (i, 0)),
        compiler_params=_params("parallel"),
        name="prenorm",
    )(x2d, gain.reshape(1, d), mod4, mod4)


def _mm_kernel(a_ref, w_ref, o_ref):
    o_ref[...] = _dot(a_ref[...], w_ref[...].astype(BF16)).astype(o_ref.dtype)


def _matmul(a, w, n_out, tm, tn, col_block, name):
    m, k = a.shape
    return pl.pallas_call(
        _mm_kernel,
        out_shape=jax.ShapeDtypeStruct((m, n_out), BF16),
        grid=(m // tm, n_out // tn),
        in_specs=[pl.BlockSpec((tm, k), lambda i, j: (i, 0), pipeline_mode=pl.Buffered(1)),
                  pl.BlockSpec((k, tn), lambda i, j: (0, col_block(j)))],
        out_specs=pl.BlockSpec((tm, tn), lambda i, j: (i, j)),
        compiler_params=_params("parallel", "arbitrary"),
        name=name,
    )(a, w)


def _na_tables(rows):
    wr = NA_WIN_R
    bases, tables = [], []
    for t in range(rows // NA_Q_ROWS):
        kb = int(np.clip(NA_Q_ROWS * t - wr // 2, 0, rows - NA_K_ROWS))
        tab = []
        for i in range(NA_Q_ROWS):
            r = NA_Q_ROWS * t + i
            r0 = int(np.clip(r - wr // 2, 0, rows - wr))
            tab.append(tuple((kb + j - r + NA_WIN_R - 1) if r0 <= kb + j < r0 + wr else None
                             for j in range(NA_K_ROWS)))
        bases.append(kb)
        tables.append(tuple(tab))
    uniq = list(dict.fromkeys(tables))
    return bases, uniq, [uniq.index(t) for t in tables]


def _na_bias_diagonals(rpb):
    offset = np.clip(np.arange(LANES) - (GRID_W - 1), -(NA_WIN_C - 1), NA_WIN_C - 1) + (NA_WIN_C - 1)
    return rpb.astype(F32)[:, :, offset]


def _na_build_bias(diag_ref, rowbias_ref, bias_ref, patterns):
    assert LANES == 2 * GRID_W
    shape = (GRID_W, LANES)
    qc = lax.broadcasted_iota(jnp.int32, shape, 0)
    lane = lax.broadcasted_iota(jnp.int32, shape, 1)
    kc = lane % GRID_W
    c0 = jnp.clip(qc - NA_WIN_C // 2, 0, GRID_W - NA_WIN_C)
    col_ok = (kc >= c0) & (kc < c0 + NA_WIN_C)
    low = lane < GRID_W
    for dr in range(diag_ref.shape[0]):
        diag = jnp.broadcast_to(diag_ref[dr:dr + 1, :], shape)
        first = pltpu.roll(diag, GRID_W + 1, 1, stride=1, stride_axis=0)
        second = pltpu.roll(diag, 1, 1, stride=1, stride_axis=0)
        rowbias_ref[dr] = jnp.where(col_ok, jnp.where(low, first, second), NEG_INF)
    outside = jnp.full(shape, NEG_INF, F32)
    block = lambda dr: outside if dr is None else rowbias_ref[dr]
    for p, tab in enumerate(patterns):
        for i, row in enumerate(tab):
            for j in range(0, NA_K_ROWS, 2):
                bias_ref[p, i * GRID_W:(i + 1) * GRID_W, j * GRID_W:(j + 2) * GRID_W] = jnp.where(
                    low, block(row[j]), block(row[j + 1]))


def _na_kernel(q_ref, k_ref, v_ref, kc_ref, vc_ref, diag_ref, o_ref, rowbias_ref, bias_ref, *, rows, patterns,
               pat_ids):
    tq = NA_Q_ROWS * GRID_W
    nk = NA_K_ROWS * GRID_W
    scale = NA_HEAD_DIM ** -0.5

    @pl.when(pl.program_id(1) == 0)
    def _():
        _na_build_bias(diag_ref, rowbias_ref, bias_ref, patterns)

    def tile(t, carry):
        kb = pl.multiple_of(jnp.clip(NA_Q_ROWS * t - NA_WIN_R // 2, 0, rows - NA_K_ROWS) * GRID_W, tq)
        q0 = pl.multiple_of(t * tq, tq)
        pat = 0
        for i, pid in enumerate(pat_ids):
            pat = jnp.where(t == i, pid, pat)
        q = q_ref[pl.ds(q0, tq), :]
        s_w = _dot_nt(q, k_ref[pl.ds(kb, nk), :]) * scale + bias_ref[pat]
        s_c = _dot_nt(q, kc_ref[...]) * scale
        m = jnp.maximum(jnp.max(s_w, axis=-1, keepdims=True), jnp.max(s_c, axis=-1, keepdims=True))
        p_w = jnp.exp(s_w - m)
        p_c = jnp.exp(s_c - m)
        l = jnp.sum(p_w, axis=-1, keepdims=True) + jnp.sum(p_c, axis=-1, keepdims=True)
        o = _dot(p_w.astype(BF16), v_ref[pl.ds(kb, nk), :]) + _dot(p_c.astype(BF16), vc_ref[...])
        o_ref[pl.ds(q0, tq), :] = (o / l).astype(o_ref.dtype)
        return carry

    lax.fori_loop(0, rows // NA_Q_ROWS, tile, 0, unroll=4)


def _na_attention(proj, cproj, rpb, batch, t_len, tc_len, col_q, col_k, col_v):
    rows = t_len // GRID_W
    tq = NA_Q_ROWS * GRID_W
    _, patterns, pat_ids = _na_tables(rows)
    diag = _na_bias_diagonals(rpb)
    n_dr = diag.shape[1]
    dh = NA_HEAD_DIM
    tok = lambda col: pl.BlockSpec((t_len, dh), lambda h, b: (b, col // dh + h))
    return pl.pallas_call(
        functools.partial(_na_kernel, rows=rows, patterns=tuple(patterns), pat_ids=tuple(pat_ids)),
        out_shape=jax.ShapeDtypeStruct((batch * t_len, NA_HEADS * dh), BF16),
        grid=(NA_HEADS, batch),
        in_specs=[tok(col_q), tok(col_k), tok(col_v),
                  pl.BlockSpec((tc_len, dh), lambda h, b: (b, h)),
                  pl.BlockSpec((tc_len, dh), lambda h, b: (b, NA_HEADS + h)),
                  pl.BlockSpec((None, n_dr, LANES), lambda h, b: (h, 0, 0))],
        out_specs=pl.BlockSpec((t_len, dh), lambda h, b: (b, h)),
        scratch_shapes=[pltpu.VMEM((n_dr, GRID_W, LANES), F32),
                        pltpu.VMEM((len(patterns), tq, NA_K_ROWS * GRID_W), F32)],
        compiler_params=_params("arbitrary", "arbitrary"),
        name="na_attention",
    )(proj, proj, proj, cproj, cproj, diag)


def _rope_tables(t_len):
    quarter = RET_DIM // 4
    inv = ROPE_BASE ** (-np.arange(quarter, dtype=np.float64) / quarter)
    tpos = np.arange(t_len)
    row_ang = (tpos // GRID_W).astype(np.float64)[:, None] * inv[None, :]
    col_ang = (tpos % GRID_W).astype(np.float64)[:, None] * inv[None, :]
    cos = np.concatenate([np.cos(row_ang)] * 2 + [np.cos(col_ang)] * 2, axis=-1)
    sin = np.concatenate([-np.sin(row_ang), np.sin(row_ang), -np.sin(col_ang), np.sin(col_ang)], axis=-1)
    return jnp.asarray(cos, F32), jnp.asarray(sin, F32)


def _rope(a, cos, sin):
    half = RET_DIM // 2
    swapped = jnp.concatenate([pltpu.roll(a[:, :half], half // 2, 1), pltpu.roll(a[:, half:], half // 2, 1)], axis=1)
    return a * cos + swapped * sin


def _ret_kernel(dec_ref, q_ref, k_ref, v_ref, gf_ref, gb_ref, kc_ref, vc_ref, cos_ref, sin_ref, o_ref,
                sf_ref, sb_ref, acc_ref, qrot_ref, krot_ref):
    h = pl.program_id(1)
    c = RET_CHUNK
    t_len = q_ref.shape[0]
    tc_len = kc_ref.shape[0]
    nc = t_len // c
    k_scale = RET_DIM ** -0.5

    def log_gamma(direction):
        e = jnp.full((1, 1), dec_ref[direction, h], F32)
        return jnp.log1p(-jnp.exp2(-e))

    lg_f, lg_b = log_gamma(0), log_gamma(1)
    pos = lax.broadcasted_iota(jnp.int32, (c, 1), 0).astype(F32)
    diff = pos - lax.broadcasted_iota(jnp.int32, (1, c), 1).astype(F32)
    dec_f = jnp.where(diff >= 0, jnp.exp(lg_f * jnp.maximum(diff, 0.0)), 0.0) * k_scale
    dec_b = jnp.where(diff <= 0, jnp.exp(lg_b * jnp.maximum(-diff, 0.0)), 0.0) * k_scale
    qdec_f, kdec_f, cdec_f = jnp.exp(lg_f * (pos + 1.0)), jnp.exp(lg_f * (c - 1.0 - pos)) * k_scale, jnp.exp(lg_f * c)
    qdec_b, kdec_b, cdec_b = jnp.exp(lg_b * (c - pos)), jnp.exp(lg_b * pos) * k_scale, jnp.exp(lg_b * c)

    cpos = lax.broadcasted_iota(jnp.int32, (tc_len, 1), 0).astype(F32)
    kc = kc_ref[...].astype(F32) * k_scale
    vc = vc_ref[...]
    sf_ref[...] = _dot_tn((kc * jnp.exp(lg_f * (tc_len - 1.0 - cpos))).astype(BF16), vc)
    sb_ref[...] = _dot_tn((kc * jnp.exp(lg_b * cpos)).astype(BF16), vc)

    def chunk(n, s_ref, dec, qdec, kdec, cdec, g_ref, first_visit):
        rows = pl.ds(pl.multiple_of(n * c, c), c)
        if first_visit:
            cos, sin = cos_ref[rows, :], sin_ref[rows, :]
            qb = _rope(q_ref[rows, :].astype(F32), cos, sin).astype(BF16)
            k = _rope(k_ref[rows, :].astype(F32), cos, sin)
            qrot_ref[rows, :] = qb
            krot_ref[rows, :] = k
        else:
            qb, k = qrot_ref[rows, :], krot_ref[rows, :]
        v = v_ref[rows, :]
        scores = _dot_nt(qb, k.astype(BF16)) * dec
        s = s_ref[...]
        o = _dot(scores.astype(BF16), v) + _dot(qb, s.astype(BF16)) * qdec
        s_ref[...] = s * cdec + _dot_tn((k * kdec).astype(BF16), v)
        on = o * lax.rsqrt(jnp.mean(o * o, axis=-1, keepdims=True) + NORM_EPS)
        gated = _silu(g_ref[rows, :].astype(F32)) * on
        if first_visit:
            acc_ref[rows, :] = gated
        else:
            acc_ref[rows, :] += gated

    assert nc % 2 == 0

    def steps(first_visit):
        def body(n, carry):
            chunk(n, sf_ref, dec_f, qdec_f, kdec_f, cdec_f, gf_ref, first_visit)
            chunk(nc - 1 - n, sb_ref, dec_b, qdec_b, kdec_b, cdec_b, gb_ref, first_visit)
            return carry
        return body

    lax.fori_loop(0, nc // 2, steps(True), 0)
    lax.fori_loop(nc // 2, nc, steps(False), 0)
    o_ref[...] = acc_ref[...].astype(o_ref.dtype)


def _retention(proj, cproj, ret_decay, batch, t_len, tc_len, col_q, col_k, col_v, col_gf, col_gb, ccol_k, ccol_v):
    d = RET_DIM
    cos, sin = _rope_tables(t_len)
    tok = lambda col: pl.BlockSpec((t_len, d), lambda b, h: (b, col // d + h))
    ctx = lambda col: pl.BlockSpec((tc_len, d), lambda b, h: (b, col // d + h))
    tab = pl.BlockSpec((t_len, d), lambda b, h: (0, 0))
    return pl.pallas_call(
        _ret_kernel,
        out_shape=jax.ShapeDtypeStruct((batch * t_len, RET_HEADS * d), BF16),
        grid=(batch, RET_HEADS),
        in_specs=[pl.BlockSpec(memory_space=pltpu.SMEM),
                  tok(col_q), tok(col_k), tok(col_v), tok(col_gf), tok(col_gb), ctx(ccol_k), ctx(ccol_v), tab, tab],
        out_specs=pl.BlockSpec((t_len, d), lambda b, h: (b, h)),
        scratch_shapes=[pltpu.VMEM((d, d), F32), pltpu.VMEM((d, d), F32), pltpu.VMEM((t_len, d), F32),
                        pltpu.VMEM((t_len, d), BF16), pltpu.VMEM((t_len, d), F32)],
        compiler_params=_params("parallel", "arbitrary"),
        name="retention",
    )(ret_decay.astype(F32), proj, proj, proj, proj, proj, cproj, cproj, cos, sin)


def _merge_kernel(ya_ref, yr_ref, wa_ref, wr_ref, ga_ref, gb_ref, o_ref):
    a = _dot(ya_ref[...], wa_ref[...].astype(BF16))
    r = _dot(yr_ref[...], wr_ref[...].astype(BF16))
    o_ref[...] = (jax.nn.sigmoid(ga_ref[...].astype(F32)) * a + jax.nn.sigmoid(gb_ref[...].astype(F32)) * r
                  ).astype(o_ref.dtype)


def _merge(y_na, y_ret, w_na, w_ret, proj, col_ga, col_gb):
    m, ka = y_na.shape
    kr = y_ret.shape[1]
    n = w_na.shape[1]
    tm, tn = 1024, 512
    return pl.pallas_call(
        _merge_kernel,
        out_shape=jax.ShapeDtypeStruct((m, n), BF16),
        grid=(m // tm, n // tn),
        in_specs=[pl.BlockSpec((tm, ka), lambda i, j: (i, 0)),
                  pl.BlockSpec((tm, kr), lambda i, j: (i, 0)),
                  pl.BlockSpec((ka, tn), lambda i, j: (0, j)),
                  pl.BlockSpec((kr, tn), lambda i, j: (0, j)),
                  pl.BlockSpec((tm, tn), lambda i, j: (i, col_ga // tn + j)),
                  pl.BlockSpec((tm, tn), lambda i, j: (i, col_gb // tn + j))],
        out_specs=pl.BlockSpec((tm, tn), lambda i, j: (i, j)),
        compiler_params=_params("parallel", "arbitrary"),
        name="merge",
    )(y_na, y_ret, w_na, w_ret, proj, proj)


def _outproj_kernel(m_ref, w_ref, x_ref, g_ref, o_ref):
    o_ref[...] = x_ref[...] + g_ref[...] * _dot(m_ref[...], w_ref[...].astype(BF16))


def _outproj(mixed, w_out, x2d, mod4, rows_per_sample, k_gate):
    m, k = mixed.shape
    n = w_out.shape[1]
    tm, tn = 1024, 512
    per = rows_per_sample // tm
    return pl.pallas_call(
        _outproj_kernel,
        out_shape=jax.ShapeDtypeStruct((m, n), F32),
        grid=(m // tm, n // tn),
        in_specs=[pl.BlockSpec((tm, k), lambda i, j: (i, 0)),
                  pl.BlockSpec((k, tn), lambda i, j: (0, j)),
                  pl.BlockSpec((tm, tn), lambda i, j: (i, j)),
                  pl.BlockSpec((None, None, 1, tn), lambda i, j: (i // per, k_gate, 0, j))],
        out_specs=pl.BlockSpec((tm, tn), lambda i, j: (i, j)),
        compiler_params=_params("parallel", "arbitrary"),
        name="outproj",
    )(mixed, w_out, x2d, mod4)


def _router_kernel(x_ref, g_ref, sh_ref, sc_ref, wr_ref, o_ref, a_ref):
    d = x_ref.shape[1]
    h = _modulated_norm(x_ref[...], g_ref[...], sh_ref[...], sc_ref[...])
    logits = _dot(h.astype(BF16), wr_ref[...])
    lane = lax.broadcasted_iota(jnp.int32, logits.shape, 1)
    logits = jnp.where(lane < N_EXPERTS, logits, NEG_INF)
    p = jnp.exp(logits - jnp.max(logits, axis=-1, keepdims=True))
    aff = p / jnp.sum(p, axis=-1, keepdims=True)
    o_ref[:, :d] = h
    o_ref[:, d:] = aff
    a_ref[...] = aff[:, :N_EXPERTS]


def _router(x2d, gain, mod4, w_router_pad, rows_per_sample, k_shift):
    r, d = x2d.shape
    tr = 512
    per = rows_per_sample // tr
    return pl.pallas_call(
        _router_kernel,
        out_shape=(jax.ShapeDtypeStruct((r, d + LANES), F32), jax.ShapeDtypeStruct((r, N_EXPERTS), F32)),
        grid=(r // tr,),
        in_specs=[pl.BlockSpec((tr, d), lambda i: (i, 0)),
                  pl.BlockSpec((1, d), lambda i: (0, 0)),
                  pl.BlockSpec((None, None, 1, d), lambda i: (i // per, k_shift, 0, 0)),
                  pl.BlockSpec((None, None, 1, d), lambda i: (i // per, k_shift + 1, 0, 0)),
                  pl.BlockSpec((d, LANES), lambda i: (0, 0))],
        out_specs=(pl.BlockSpec((tr, d + LANES), lambda i: (i, 0)), pl.BlockSpec((tr, N_EXPERTS), lambda i: (i, 0))),
        compiler_params=_params("parallel"),
        name="router",
    )(x2d, gain.reshape(1, d), mod4, mod4, w_router_pad)


TOPK_TILE = 256
GEOMETRIC_STEPS = 32
ARITHMETIC_STEPS = 12


def _topk_kernel(aff_ref, affc_ref, slot_ref, idx_ref, bounds_ref, slot_t_ref, *, cap):
    t_len = aff_ref.shape[0]
    tt = TOPK_TILE
    nt = t_len // tt
    packed = affc_ref[...]

    def per_expert(v):
        shift = LANES // 2
        while shift >= N_EXPERTS:
            v = v + pltpu.roll(v, shift, 1)
            shift //= 2
        return v

    def narrow(c, mid):
        lo, hi = c
        ge = per_expert(jnp.sum(jnp.where(packed >= mid, 1.0, 0.0), axis=0, keepdims=True)) >= cap
        return jnp.where(ge, mid, lo), jnp.where(ge, hi, mid)

    tiny = float(np.finfo(np.float32).tiny)
    above_tiny = per_expert(jnp.sum(jnp.where(packed >= tiny, 1.0, 0.0), axis=0, keepdims=True)) >= cap
    bracket = (jnp.where(above_tiny, tiny, 0.0), jnp.where(above_tiny, 2.0, tiny) + jnp.zeros((1, LANES), F32))
    geometric_mid = lambda c: jnp.clip(jnp.sqrt(c[0]) * jnp.sqrt(c[1]), c[0], c[1])
    bracket = lax.fori_loop(0, GEOMETRIC_STEPS, lambda _, c: narrow(c, geometric_mid(c)), bracket)
    lo, hi = lax.fori_loop(0, ARITHMETIC_STEPS, lambda _, c: narrow(c, 0.5 * (c[0] + c[1])), bracket)

    def count_above(i, cnt):
        r0 = pl.multiple_of(i * tt, tt)
        return cnt + jnp.sum(jnp.where(aff_ref[pl.ds(r0, tt), :] >= hi, 1.0, 0.0), axis=0, keepdims=True)
    need = cap - lax.fori_loop(0, nt, count_above, jnp.zeros((1, LANES), F32))

    tri = jnp.where(lax.broadcasted_iota(jnp.int32, (tt, tt), 0) >= lax.broadcasted_iota(jnp.int32, (tt, tt), 1),
                    1.0, 0.0).astype(BF16)

    def assign(i, carry):
        eq_before, sel_before = carry
        r0 = pl.multiple_of(i * tt, tt)
        a = aff_ref[pl.ds(r0, tt), :]
        above = a >= hi
        tie = (a >= lo) & (a < hi)
        eq = jnp.where(tie, 1.0, 0.0)
        eq_rank = _dot(tri, eq.astype(BF16)) + eq_before
        sel = jnp.where(above | (tie & (eq_rank <= need)), 1.0, 0.0)
        sel_rank = _dot(tri, sel.astype(BF16)) + sel_before
        slot_ref[pl.ds(r0, tt), :] = jnp.where(sel > 0, sel_rank - 1.0, -1.0).astype(jnp.int32)
        bounds_ref[pl.ds(i, 1), :] = sel_before.astype(jnp.int32)
        return (eq_before + jnp.sum(eq, axis=0, keepdims=True), sel_before + jnp.sum(sel, axis=0, keepdims=True))

    zero = jnp.zeros((1, LANES), F32)
    _, total = lax.fori_loop(0, nt, assign, (zero, zero))
    bounds_ref[nt:nt + 1, :] = total.astype(jnp.int32)

    slot_t_ref[...] = jnp.transpose(slot_ref[...].astype(F32))
    idx_ref[...] = jnp.zeros_like(idx_ref)
    sublanes = 8
    tok = lax.broadcasted_iota(jnp.int32, (sublanes, t_len), 1).astype(F32)
    sub = lax.broadcasted_iota(jnp.int32, (sublanes, 1), 0).astype(F32)
    for e in range(N_EXPERTS):
        def body(g, carry):
            s0 = pl.multiple_of(g * sublanes, sublanes)
            hit = slot_t_ref[e:e + 1, :] == sub + s0.astype(F32)
            idx_ref[pl.ds(s0, sublanes), e:e + 1] = jnp.sum(jnp.where(hit, tok, 0.0), axis=1,
                                                            keepdims=True).astype(jnp.int32)
            return carry
        lax.fori_loop(0, cap // sublanes, body, 0, unroll=8)


def _topk(hext, aff, batch, t_len, d, cap):
    nb = t_len // TOPK_TILE + 1
    packed_rows = t_len * N_EXPERTS // LANES
    slot, idx_t, bounds = pl.pallas_call(
        functools.partial(_topk_kernel, cap=cap),
        out_shape=(jax.ShapeDtypeStruct((batch * t_len, LANES), jnp.int32),
                   jax.ShapeDtypeStruct((batch, cap, LANES), jnp.int32),
                   jax.ShapeDtypeStruct((batch, nb, LANES), jnp.int32)),
        grid=(batch,),
        in_specs=[pl.BlockSpec((t_len, LANES), lambda b: (b, d // LANES)),
                  pl.BlockSpec((None, packed_rows, LANES), lambda b: (b, 0, 0))],
        out_specs=(pl.BlockSpec((t_len, LANES), lambda b: (b, 0)),
                   pl.BlockSpec((None, cap, LANES), lambda b: (b, 0, 0)),
                   pl.BlockSpec((None, nb, LANES), lambda b: (b, 0, 0))),
        scratch_shapes=[pltpu.VMEM((LANES, t_len), F32)],
        compiler_params=_params("parallel"),
        name="topk",
    )(hext, aff.reshape(batch, packed_rows, LANES))
    return slot, idx_t[:, :, :N_EXPERTS].transpose(0, 2, 1), bounds


GATHER_ROWS = 128


def _gather_kernel(idx_ref, h_hbm, xe_ref, g_ref, buf, sem, *, t_len, d, nchunk):
    step = pl.program_id(0)
    rc = GATHER_ROWS
    cur = step % 2

    def request(st, slot):
        lst, chunk = st // nchunk, st % nchunk
        row0 = (lst // N_EXPERTS) * t_len

        def issue(r, carry):
            row = row0 + idx_ref[lst, chunk * rc + r]
            pltpu.make_async_copy(h_hbm.at[pl.ds(row, 1)], buf.at[slot, pl.ds(r, 1)], sem.at[slot]).start()
            return carry

        lax.fori_loop(0, rc, issue, 0, unroll=8)

    @pl.when(step == 0)
    def _():
        request(0, 0)

    @pl.when(step + 1 < pl.num_programs(0))
    def _():
        request(step + 1, 1 - cur)

    pltpu.make_async_copy(h_hbm.at[pl.ds(0, rc)], buf.at[cur], sem.at[cur]).wait()
    rows = buf[cur]
    xe_ref[...] = rows[:, :d].astype(xe_ref.dtype)
    aff = rows[:, d:]
    e = (step // nchunk) % N_EXPERTS
    lane = lax.broadcasted_iota(jnp.int32, aff.shape, 1)
    g_ref[...] = jnp.broadcast_to(jnp.sum(jnp.where(lane == e, aff, 0.0), axis=1, keepdims=True), aff.shape)


def _gather(idx, hext, batch, t_len, d, cap):
    rc = GATHER_ROWS
    nchunk = cap // rc

    def out_block(s, idx):
        lst = s // nchunk
        return lst % N_EXPERTS, (lst // N_EXPERTS) * nchunk + s % nchunk, 0

    return pl.pallas_call(
        functools.partial(_gather_kernel, t_len=t_len, d=d, nchunk=nchunk),
        out_shape=(jax.ShapeDtypeStruct((N_EXPERTS, batch * cap, d), BF16),
                   jax.ShapeDtypeStruct((N_EXPERTS, batch * cap, LANES), F32)),
        grid_spec=pltpu.PrefetchScalarGridSpec(
            num_scalar_prefetch=1,
            grid=(batch * N_EXPERTS * nchunk,),
            in_specs=[pl.BlockSpec(memory_space=pl.ANY)],
            out_specs=(pl.BlockSpec((None, rc, d), out_block), pl.BlockSpec((None, rc, LANES), out_block)),
            scratch_shapes=[pltpu.VMEM((2, rc, d + LANES), F32), pltpu.SemaphoreType.DMA((2,))]),
        compiler_params=_params("arbitrary"),
        name="gather",
    )(idx.reshape(batch * N_EXPERTS, cap), hext)


def _expert_up_kernel(x_ref, wg_ref, wu_ref, o_ref):
    x = x_ref[...]
    a = _dot(x, wg_ref[...].astype(BF16))
    u = _dot(x, wu_ref[...].astype(BF16))
    o_ref[...] = (_silu(a) * u).astype(o_ref.dtype)


def _expert_up(xe, w_gate, w_up):
    e, m, d = xe.shape
    ff = w_gate.shape[2]
    tf = 256
    return pl.pallas_call(
        _expert_up_kernel,
        out_shape=jax.ShapeDtypeStruct((e, m, ff), BF16),
        grid=(e, ff // tf),
        in_specs=[pl.BlockSpec((None, m, d), lambda i, f: (i, 0, 0)),
                  pl.BlockSpec((None, d, tf), lambda i, f: (i, 0, f)),
                  pl.BlockSpec((None, d, tf), lambda i, f: (i, 0, f))],
        out_specs=pl.BlockSpec((None, m, tf), lambda i, f: (i, 0, f)),
        compiler_params=_params("parallel", "arbitrary"),
        name="expert_up",
    )(xe, w_gate, w_up)


def _expert_down_kernel(a_ref, w_ref, g_ref, o_ref):
    o_ref[...] = (_dot(a_ref[...], w_ref[...].astype(BF16)) * g_ref[:, :1]).astype(o_ref.dtype)


def _expert_down(act, w_down, g):
    e, m, ff = act.shape
    d = w_down.shape[2]
    tn = min(1024, d)
    return pl.pallas_call(
        _expert_down_kernel,
        out_shape=jax.ShapeDtypeStruct((e, m, d), BF16),
        grid=(e, d // tn),
        in_specs=[pl.BlockSpec((None, m, ff), lambda i, j: (i, 0, 0)),
                  pl.BlockSpec((None, ff, tn), lambda i, j: (i, 0, j)),
                  pl.BlockSpec((None, m, LANES), lambda i, j: (i, 0, 0))],
        out_specs=pl.BlockSpec((None, m, tn), lambda i, j: (i, 0, j)),
        compiler_params=_params("parallel", "arbitrary"),
        name="expert_down",
    )(act, w_down, g)


COMBINE_WINDOW = 64
ROW_ALIGN = 16


def _combine_kernel(bounds_ref, slot_ref, ye_hbm, x_ref, g_ref, fn_ref, o_ref, stage, onehot, sem, *, nt, cap):
    step = pl.program_id(0)
    w = COMBINE_WINDOW
    cur = step % 2

    def tile_rows(st, e):
        b, i = st // nt, st % nt
        return b, bounds_ref[b * (nt + 1) + i, e], bounds_ref[b * (nt + 1) + i + 1, e]

    def window(st, e, r):
        b, first, _ = tile_rows(st, e)
        base = (first // ROW_ALIGN) * ROW_ALIGN + r * w
        return b, base, jnp.minimum(base, cap - w)

    def window_copy(st, e, r, buf):
        b, _, src = window(st, e, r)
        return pltpu.make_async_copy(ye_hbm.at[e, pl.ds(pl.multiple_of(b * cap + src, ROW_ALIGN), w), :],
                                     stage.at[buf, pl.ds(e * w, w), :], sem.at[buf])

    def start_round(st, r, buf):
        for e in range(N_EXPERTS):
            window_copy(st, e, r, buf).start()

    def wait_round(st, r, buf):
        for e in range(N_EXPERTS):
            window_copy(st, e, r, buf).wait()

    def scatter(r):
        slots = slot_ref[...]
        pos = lax.broadcasted_iota(jnp.int32, (1, w), 1)
        for e in range(N_EXPERTS):
            _, base, src = window(step, e, r)
            col = slots[:, e:e + 1]
            hit = (col >= base) & (col - src == pos)
            onehot[:, e * w:(e + 1) * w] = jnp.where(hit, 1.0, 0.0).astype(BF16)
        return _dot(onehot[...], stage[cur])

    @pl.when(step == 0)
    def _():
        start_round(0, 0, 0)

    @pl.when(step + 1 < pl.num_programs(0))
    def _():
        start_round(step + 1, 0, 1 - cur)

    wait_round(step, 0, cur)
    o_ref[...] = scatter(0)

    rounds = 1
    for e in range(N_EXPERTS):
        _, first, last = tile_rows(step, e)
        rounds = jnp.maximum(rounds, (last - (first // ROW_ALIGN) * ROW_ALIGN + w - 1) // w)

    def extra_round(r, carry):
        start_round(step, r, cur)
        wait_round(step, r, cur)
        o_ref[...] += scatter(r)
        return carry

    lax.fori_loop(1, rounds, extra_round, 0)

    v = x_ref[...] + g_ref[...] * o_ref[...]
    y = v * lax.rsqrt(jnp.mean(v * v, axis=-1, keepdims=True) + NORM_EPS)
    o_ref[...] = y * fn_ref[...]


def _combine(bounds, slot, ye, x2d, mod4, final_norm, batch, t_len, cap, k_gate):
    m, d = x2d.shape
    tm = TOPK_TILE
    nt = t_len // tm
    w = COMBINE_WINDOW
    return pl.pallas_call(
        functools.partial(_combine_kernel, nt=nt, cap=cap),
        out_shape=jax.ShapeDtypeStruct((m, d), F32),
        grid_spec=pltpu.PrefetchScalarGridSpec(
            num_scalar_prefetch=1,
            grid=(batch * nt,),
            in_specs=[pl.BlockSpec((tm, LANES), lambda s, bnd: (s, 0)),
                      pl.BlockSpec(memory_space=pl.ANY),
                      pl.BlockSpec((tm, d), lambda s, bnd: (s, 0)),
                      pl.BlockSpec((None, None, 1, d), lambda s, bnd: (s // nt, k_gate, 0, 0)),
                      pl.BlockSpec((1, d), lambda s, bnd: (0, 0))],
            out_specs=pl.BlockSpec((tm, d), lambda s, bnd: (s, 0)),
            scratch_shapes=[pltpu.VMEM((2, N_EXPERTS * w, d), BF16), pltpu.VMEM((tm, N_EXPERTS * w), BF16),
                            pltpu.SemaphoreType.DMA((2,))]),
        compiler_params=_params("arbitrary"),
        name="combine",
    )(bounds.reshape(batch * (nt + 1), LANES), slot, ye, x2d, mod4, final_norm.reshape(1, d))


def kernel(x, c, ctx, c_ctx, norm1, norm2, w_mod, b_mod, w_in, na_rpb, ret_decay, w_branch_na, w_branch_ret,
           w_out, w_router, w_gate, w_up, w_down, final_norm):
    batch, t_len, d = x.shape
    tc_len = ctx.shape[1]
    na_w = NA_HEADS * NA_HEAD_DIM
    ret_w = RET_HEADS * RET_DIM
    col_qa, col_ka, col_va = 0, na_w, 2 * na_w
    col_qr = 3 * na_w
    col_kr, col_vr, col_gf, col_gb = col_qr + ret_w, col_qr + 2 * ret_w, col_qr + 3 * ret_w, col_qr + 4 * ret_w
    col_ga = col_qr + 5 * ret_w
    col_gb2 = col_ga + d
    cap = EC_CAPACITY_FACTOR * t_len // N_EXPERTS
    assert w_in.shape[0] == 1, "single layer"

    x2d = x.reshape(batch * t_len, d)
    cvec = jnp.concatenate([c, c_ctx[None], jnp.zeros((8 - batch - 1, d), F32)], axis=0)
    mod = _modulation(cvec, w_mod[0], b_mod[0])
    mod4 = mod[:batch + 1].reshape(batch + 1, N_MOD, 1, d)

    h = _prenorm(x2d, norm1[0], mod4, t_len, 0, 0)
    hc = _prenorm(ctx.reshape(batch * tc_len, d), norm1[0], mod4, batch * tc_len, batch, 0)
    tn = 512
    proj = _matmul(h, w_in[0], w_in.shape[2], 2048, tn, lambda j: j, "in_proj")
    kv_w = 2 * na_w
    cproj = _matmul(hc, w_in[0], kv_w + 2 * ret_w, batch * tc_len, tn,
                    lambda j: jnp.where(j < kv_w // tn, col_ka // tn + j, col_kr // tn + j - kv_w // tn), "ctx_proj")

    y_na = _na_attention(proj, cproj, na_rpb[0], batch, t_len, tc_len, col_qa, col_ka, col_va)
    y_ret = _retention(proj, cproj, ret_decay[0], batch, t_len, tc_len, col_qr, col_kr, col_vr, col_gf, col_gb,
                       kv_w, kv_w + ret_w)
    mixed = _merge(y_na, y_ret, w_branch_na[0], w_branch_ret[0], proj, col_ga, col_gb2)
    x1 = _outproj(mixed, w_out[0], x2d, mod4, t_len, 2)

    w_router_pad = jnp.pad(w_router[0], ((0, 0), (0, LANES - N_EXPERTS))).astype(BF16)
    hext, aff = _router(x1, norm2[0], mod4, w_router_pad, t_len, 3)
    slot, idx, bounds = _topk(hext, aff, batch, t_len, d, cap)
    xe, g = _gather(idx, hext, batch, t_len, d, cap)
    act = _expert_up(xe, w_gate[0], w_up[0])
    ye = _expert_down(act, w_down[0], g)
    out = _combine(bounds, slot, ye, x1, mod4, final_norm, batch, t_len, cap, 5)
    return out.reshape(batch, t_len, d)
```

```python
import functools

import numpy as np
import jax
import jax.numpy as jnp
from jax import lax
from jax.experimental import pallas as pl
from jax.experimental.pallas import tpu as pltpu

F32 = jnp.float32
BF16 = jnp.bfloat16

GRID_W = 64
NA_HEADS = 16
NA_HEAD_DIM = 128
NA_WIN_R = 8
NA_WIN_C = 16
RET_HEADS = 8
RET_DIM = 256
RET_CHUNK = 256
N_EXPERTS = 16
EC_CAPACITY_FACTOR = 2
ROPE_BASE = 10000.0
NORM_EPS = 1e-6
NEG_INF = -1e30
N_MOD = 6

VMEM_LIMIT_BYTES = 56 * 1024 * 1024
LANES = 128

NA_Q_ROWS = 4
NA_K_ROWS = NA_Q_ROWS + NA_WIN_R


def _params(*sem):
    return pltpu.CompilerParams(dimension_semantics=sem, vmem_limit_bytes=VMEM_LIMIT_BYTES)


def _dot(a, b):
    return jnp.dot(a, b, preferred_element_type=F32)


def _dot_nt(a, b):
    return lax.dot_general(a, b, (((1,), (1,)), ((), ())), preferred_element_type=F32)


def _dot_tn(a, b):
    return lax.dot_general(a, b, (((0,), (0,)), ((), ())), preferred_element_type=F32)


def _silu(x):
    return x * jax.nn.sigmoid(x)


def _mod_kernel(c_ref, w_ref, b_ref, o_ref):
    a = _silu(c_ref[...]).astype(BF16)
    o_ref[...] = _dot(a, w_ref[...].astype(BF16)) + b_ref[...]


def _modulation(cvec, w_mod, b_mod):
    r, d = cvec.shape
    n = w_mod.shape[1]
    tn = 1024
    return pl.pallas_call(
        _mod_kernel,
        out_shape=jax.ShapeDtypeStruct((r, n), F32),
        grid=(n // tn,),
        in_specs=[pl.BlockSpec((r, d), lambda j: (0, 0)),
                  pl.BlockSpec((d, tn), lambda j: (0, j)),
                  pl.BlockSpec((1, tn), lambda j: (0, j))],
        out_specs=pl.BlockSpec((r, tn), lambda j: (0, j)),
        compiler_params=_params("arbitrary"),
        name="modulation",
    )(cvec, w_mod, b_mod.reshape(1, n))


def _modulated_norm(x, g, shift, scale):
    y = x * lax.rsqrt(jnp.mean(x * x, axis=-1, keepdims=True) + NORM_EPS)
    return (y * g) * (1.0 + scale) + shift


def _prenorm_kernel(x_ref, g_ref, sh_ref, sc_ref, o_ref):
    o_ref[...] = _modulated_norm(x_ref[...], g_ref[...], sh_ref[...], sc_ref[...]).astype(o_ref.dtype)


def _prenorm(x2d, gain, mod4, rows_per_sample, sample0, k_shift):
    r, d = x2d.shape
    tr = 512
    per = rows_per_sample // tr
    return pl.pallas_call(
        _prenorm_kernel,
        out_shape=jax.ShapeDtypeStruct((r, d), BF16),
        grid=(r // tr,),
        in_specs=[pl.BlockSpec((tr, d), lambda i: (i, 0)),
                  pl.BlockSpec((1, d), lambda i: (0, 0)),
                  pl.BlockSpec((None, None, 1, d), lambda i: (sample0 + i // per, k_shift, 0, 0)),
                  pl.BlockSpec((None, None, 1, d), lambda i: (sample0 + i // per, k_shift + 1, 0, 0))],
        out_specs=pl.BlockSpec((tr, d), lambda i: (i, 0)),
        compiler_params=_params("parallel"),
        name="prenorm",
    )(x2d, gain.reshape(1, d), mod4, mod4)


def _mm_kernel(a_ref, w_ref, o_ref):
    o_ref[...] = _dot(a_ref[...], w_ref[...].astype(BF16)).astype(o_ref.dtype)


def _matmul(a, w, n_out, tm, tn, col_block, name):
    m, k = a.shape
    return pl.pallas_call(
        _mm_kernel,
        out_shape=jax.ShapeDtypeStruct((m, n_out), BF16),
        grid=(m // tm, n_out // tn),
        in_specs=[pl.BlockSpec((tm, k), lambda i, j: (i, 0), pipeline_mode=pl.Buffered(1)),
                  pl.BlockSpec((k, tn), lambda i, j: (0, col_block(j)))],
        out_specs=pl.BlockSpec((tm, tn), lambda i, j: (i, j)),
        compiler_params=_params("parallel", "arbitrary"),
        name=name,
    )(a, w)


def _na_tables(rows):
    wr = NA_WIN_R
    bases, tables = [], []
    for t in range(rows // NA_Q_ROWS):
        kb = int(np.clip(NA_Q_ROWS * t - wr // 2, 0, rows - NA_K_ROWS))
        tab = []
        for i in range(NA_Q_ROWS):
            r = NA_Q_ROWS * t + i
            r0 = int(np.clip(r - wr // 2, 0, rows - wr))
            tab.append(tuple((kb + j - r + NA_WIN_R - 1) if r0 <= kb + j < r0 + wr else None
                             for j in range(NA_K_ROWS)))
        bases.append(kb)
        tables.append(tuple(tab))
    uniq = list(dict.fromkeys(tables))
    return bases, uniq, [uniq.index(t) for t in tables]


def _na_bias_diagonals(rpb):
    offset = np.clip(np.arange(LANES) - (GRID_W - 1), -(NA_WIN_C - 1), NA_WIN_C - 1) + (NA_WIN_C - 1)
    return rpb.astype(F32)[:, :, offset]


def _na_build_bias(diag_ref, rowbias_ref, bias_ref, patterns):
    assert LANES == 2 * GRID_W
    shape = (GRID_W, LANES)
    qc = lax.broadcasted_iota(jnp.int32, shape, 0)
    lane = lax.broadcasted_iota(jnp.int32, shape, 1)
    kc = lane % GRID_W
    c0 = jnp.clip(qc - NA_WIN_C // 2, 0, GRID_W - NA_WIN_C)
    col_ok = (kc >= c0) & (kc < c0 + NA_WIN_C)
    low = lane < GRID_W
    for dr in range(diag_ref.shape[0]):
        diag = jnp.broadcast_to(diag_ref[dr:dr + 1, :], shape)
        first = pltpu.roll(diag, GRID_W + 1, 1, stride=1, stride_axis=0)
        second = pltpu.roll(diag, 1, 1, stride=1, stride_axis=0)
        rowbias_ref[dr] = jnp.where(col_ok, jnp.where(low, first, second), NEG_INF)
    outside = jnp.full(shape, NEG_INF, F32)
    block = lambda dr: outside if dr is None else rowbias_ref[dr]
    for p, tab in enumerate(patterns):
        for i, row in enumerate(tab):
            for j in range(0, NA_K_ROWS, 2):
                bias_ref[p, i * GRID_W:(i + 1) * GRID_W, j * GRID_W:(j + 2) * GRID_W] = jnp.where(
                    low, block(row[j]), block(row[j + 1]))


def _na_kernel(q_ref, k_ref, v_ref, kc_ref, vc_ref, diag_ref, o_ref, rowbias_ref, bias_ref, vext_ref, vcext_ref, *,
               rows, patterns, pat_ids):
    tq = NA_Q_ROWS * GRID_W
    nk = NA_K_ROWS * GRID_W
    scale = NA_HEAD_DIM ** -0.5
    for src, dst in ((v_ref, vext_ref), (vc_ref, vcext_ref)):
        dst[:, :NA_HEAD_DIM] = src[...]
        dst[:, NA_HEAD_DIM:] = jnp.ones_like(src)

    @pl.when(pl.program_id(1) == 0)
    def _():
        _na_build_bias(diag_ref, rowbias_ref, bias_ref, patterns)

    def tile(t, carry):
        kb = pl.multiple_of(jnp.clip(NA_Q_ROWS * t - NA_WIN_R // 2, 0, rows - NA_K_ROWS) * GRID_W, tq)
        q0 = pl.multiple_of(t * tq, tq)
        pat = 0
        for i, pid in enumerate(pat_ids):
            pat = jnp.where(t == i, pid, pat)
        q = q_ref[pl.ds(q0, tq), :]
        s_w = _dot_nt(q, k_ref[pl.ds(kb, nk), :]) * scale + bias_ref[pat]
        s_c = _dot_nt(q, kc_ref[...]) * scale
        m = jnp.maximum(jnp.max(s_w, axis=-1, keepdims=True), jnp.max(s_c, axis=-1, keepdims=True))
        p_w = jnp.exp(s_w - m)
        p_c = jnp.exp(s_c - m)
        o = _dot(p_w.astype(BF16), vext_ref[pl.ds(kb, nk), :]) + _dot(p_c.astype(BF16), vcext_ref[...])
        dh = NA_HEAD_DIM
        o_ref[pl.ds(q0, tq), :] = (o[:, :dh] / o[:, dh:dh + 1]).astype(o_ref.dtype)
        return carry

    lax.fori_loop(0, rows // NA_Q_ROWS, tile, 0, unroll=4)


def _na_attention(proj, cproj, rpb, batch, t_len, tc_len, col_q, col_k, col_v):
    rows = t_len // GRID_W
    tq = NA_Q_ROWS * GRID_W
    _, patterns, pat_ids = _na_tables(rows)
    diag = _na_bias_diagonals(rpb)
    n_dr = diag.shape[1]
    dh = NA_HEAD_DIM
    tok = lambda col: pl.BlockSpec((t_len, dh), lambda h, b: (b, col // dh + h))
    return pl.pallas_call(
        functools.partial(_na_kernel, rows=rows, patterns=tuple(patterns), pat_ids=tuple(pat_ids)),
        out_shape=jax.ShapeDtypeStruct((batch * t_len, NA_HEADS * dh), BF16),
        grid=(NA_HEADS, batch),
        in_specs=[tok(col_q), tok(col_k), tok(col_v),
                  pl.BlockSpec((tc_len, dh), lambda h, b: (b, h)),
                  pl.BlockSpec((tc_len, dh), lambda h, b: (b, NA_HEADS + h)),
                  pl.BlockSpec((None, n_dr, LANES), lambda h, b: (h, 0, 0))],
        out_specs=pl.BlockSpec((t_len, dh), lambda h, b: (b, h)),
        scratch_shapes=[pltpu.VMEM((n_dr, GRID_W, LANES), F32),
                        pltpu.VMEM((len(patterns), tq, NA_K_ROWS * GRID_W), F32),
                        pltpu.VMEM((t_len, 2 * dh), BF16), pltpu.VMEM((tc_len, 2 * dh), BF16)],
        compiler_params=_params("arbitrary", "arbitrary"),
        name="na_attention",
    )(proj, proj, proj, cproj, cproj, diag)


def _rope_tables(t_len):
    quarter = RET_DIM // 4
    inv = ROPE_BASE ** (-np.arange(quarter, dtype=np.float64) / quarter)
    tpos = np.arange(t_len)
    row_ang = (tpos // GRID_W).astype(np.float64)[:, None] * inv[None, :]
    col_ang = (tpos % GRID_W).astype(np.float64)[:, None] * inv[None, :]
    cos = np.concatenate([np.cos(row_ang)] * 2 + [np.cos(col_ang)] * 2, axis=-1)
    sin = np.concatenate([-np.sin(row_ang), np.sin(row_ang), -np.sin(col_ang), np.sin(col_ang)], axis=-1)
    return jnp.asarray(cos, F32), jnp.asarray(sin, F32)


def _rope(a, cos, sin):
    half = RET_DIM // 2
    swapped = jnp.concatenate([pltpu.roll(a[:, :half], half // 2, 1), pltpu.roll(a[:, half:], half // 2, 1)], axis=1)
    return a * cos + swapped * sin


def _ret_kernel(dec_ref, q_ref, k_ref, v_ref, gf_ref, gb_ref, kc_ref, vc_ref, cos_ref, sin_ref, o_ref,
                sf_ref, sb_ref, acc_ref, qrot_ref, krot_ref):
    h = pl.program_id(1)
    c = RET_CHUNK
    t_len = q_ref.shape[0]
    tc_len = kc_ref.shape[0]
    nc = t_len // c
    k_scale = RET_DIM ** -0.5

    def log_gamma(direction):
        e = jnp.full((1, 1), dec_ref[direction, h], F32)
        return jnp.log1p(-jnp.exp2(-e))

    lg_f, lg_b = log_gamma(0), log_gamma(1)
    pos = lax.broadcasted_iota(jnp.int32, (c, 1), 0).astype(F32)
    diff = pos - lax.broadcasted_iota(jnp.int32, (1, c), 1).astype(F32)
    dec_f = jnp.where(diff >= 0, jnp.exp(lg_f * jnp.maximum(diff, 0.0)), 0.0) * k_scale
    dec_b = jnp.where(diff <= 0, jnp.exp(lg_b * jnp.maximum(-diff, 0.0)), 0.0) * k_scale
    qdec_f, kdec_f, cdec_f = jnp.exp(lg_f * (pos + 1.0)), jnp.exp(lg_f * (c - 1.0 - pos)) * k_scale, jnp.exp(lg_f * c)
    qdec_b, kdec_b, cdec_b = jnp.exp(lg_b * (c - pos)), jnp.exp(lg_b * pos) * k_scale, jnp.exp(lg_b * c)

    cpos = lax.broadcasted_iota(jnp.int32, (tc_len, 1), 0).astype(F32)
    kc = kc_ref[...].astype(F32) * k_scale
    vc = vc_ref[...]
    sf_ref[...] = _dot_tn((kc * jnp.exp(lg_f * (tc_len - 1.0 - cpos))).astype(BF16), vc)
    sb_ref[...] = _dot_tn((kc * jnp.exp(lg_b * cpos)).astype(BF16), vc)

    def chunk(n, s_ref, dec, qdec, kdec, cdec, g_ref, first_visit):
        rows = pl.ds(pl.multiple_of(n * c, c), c)
        if first_visit:
            cos, sin = cos_ref[rows, :], sin_ref[rows, :]
            qb = _rope(q_ref[rows, :].astype(F32), cos, sin).astype(BF16)
            k = _rope(k_ref[rows, :].astype(F32), cos, sin)
            qrot_ref[rows, :] = qb
            krot_ref[rows, :] = k
        else:
            qb, k = qrot_ref[rows, :], krot_ref[rows, :]
        v = v_ref[rows, :]
        scores = _dot_nt(qb, k.astype(BF16)) * dec
        s = s_ref[...]
        o = _dot(scores.astype(BF16), v) + _dot(qb, s.astype(BF16)) * qdec
        s_ref[...] = s * cdec + _dot_tn((k * kdec).astype(BF16), v)
        on = o * lax.rsqrt(jnp.mean(o * o, axis=-1, keepdims=True) + NORM_EPS)
        gated = _silu(g_ref[rows, :].astype(F32)) * on
        if first_visit:
            acc_ref[rows, :] = gated
        else:
            acc_ref[rows, :] += gated

    assert nc % 2 == 0

    def steps(first_visit):
        def body(n, carry):
            chunk(n, sf_ref, dec_f, qdec_f, kdec_f, cdec_f, gf_ref, first_visit)
            chunk(nc - 1 - n, sb_ref, dec_b, qdec_b, kdec_b, cdec_b, gb_ref, first_visit)
            return carry
        return body

    lax.fori_loop(0, nc // 2, steps(True), 0)
    lax.fori_loop(nc // 2, nc, steps(False), 0)
    o_ref[...] = acc_ref[...].astype(o_ref.dtype)


def _retention(proj, cproj, ret_decay, batch, t_len, tc_len, col_q, col_k, col_v, col_gf, col_gb, ccol_k, ccol_v):
    d = RET_DIM
    cos, sin = _rope_tables(t_len)
    tok = lambda col: pl.BlockSpec((t_len, d), lambda b, h: (b, col // d + h))
    ctx = lambda col: pl.BlockSpec((tc_len, d), lambda b, h: (b, col // d + h))
    tab = pl.BlockSpec((t_len, d), lambda b, h: (0, 0))
    return pl.pallas_call(
        _ret_kernel,
        out_shape=jax.ShapeDtypeStruct((batch * t_len, RET_HEADS * d), BF16),
        grid=(batch, RET_HEADS),
        in_specs=[pl.BlockSpec(memory_space=pltpu.SMEM),
                  tok(col_q), tok(col_k), tok(col_v), tok(col_gf), tok(col_gb), ctx(ccol_k), ctx(ccol_v), tab, tab],
        out_specs=pl.BlockSpec((t_len, d), lambda b, h: (b, h)),
        scratch_shapes=[pltpu.VMEM((d, d), F32), pltpu.VMEM((d, d), F32), pltpu.VMEM((t_len, d), F32),
                        pltpu.VMEM((t_len, d), BF16), pltpu.VMEM((t_len, d), F32)],
        compiler_params=_params("parallel", "arbitrary"),
        name="retention",
    )(ret_decay.astype(F32), proj, proj, proj, proj, proj, cproj, cproj, cos, sin)


def _merge_kernel(ya_ref, yr_ref, wa_ref, wr_ref, ga_ref, gb_ref, o_ref):
    a = _dot(ya_ref[...], wa_ref[...].astype(BF16))
    r = _dot(yr_ref[...], wr_ref[...].astype(BF16))
    o_ref[...] = (jax.nn.sigmoid(ga_ref[...].astype(F32)) * a + jax.nn.sigmoid(gb_ref[...].astype(F32)) * r
                  ).astype(o_ref.dtype)


def _merge(y_na, y_ret, w_na, w_ret, proj, col_ga, col_gb):
    m, ka = y_na.shape
    kr = y_ret.shape[1]
    n = w_na.shape[1]
    tm, tn = 1024, 512
    return pl.pallas_call(
        _merge_kernel,
        out_shape=jax.ShapeDtypeStruct((m, n), BF16),
        grid=(m // tm, n // tn),
        in_specs=[pl.BlockSpec((tm, ka), lambda i, j: (i, 0)),
                  pl.BlockSpec((tm, kr), lambda i, j: (i, 0)),
                  pl.BlockSpec((ka, tn), lambda i, j: (0, j)),
                  pl.BlockSpec((kr, tn), lambda i, j: (0, j)),
                  pl.BlockSpec((tm, tn), lambda i, j: (i, col_ga // tn + j)),
                  pl.BlockSpec((tm, tn), lambda i, j: (i, col_gb // tn + j))],
        out_specs=pl.BlockSpec((tm, tn), lambda i, j: (i, j)),
        compiler_params=_params("parallel", "arbitrary"),
        name="merge",
    )(y_na, y_ret, w_na, w_ret, proj, proj)


def _outproj_kernel(m_ref, w_ref, x_ref, g_ref, o_ref):
    o_ref[...] = x_ref[...] + g_ref[...] * _dot(m_ref[...], w_ref[...].astype(BF16))


def _outproj(mixed, w_out, x2d, mod4, rows_per_sample, k_gate):
    m, k = mixed.shape
    n = w_out.shape[1]
    tm, tn = 1024, 512
    per = rows_per_sample // tm
    return pl.pallas_call(
        _outproj_kernel,
        out_shape=jax.ShapeDtypeStruct((m, n), F32),
        grid=(m // tm, n // tn),
        in_specs=[pl.BlockSpec((tm, k), lambda i, j: (i, 0)),
                  pl.BlockSpec((k, tn), lambda i, j: (0, j)),
                  pl.BlockSpec((tm, tn), lambda i, j: (i, j)),
                  pl.BlockSpec((None, None, 1, tn), lambda i, j: (i // per, k_gate, 0, j))],
        out_specs=pl.BlockSpec((tm, tn), lambda i, j: (i, j)),
        compiler_params=_params("parallel", "arbitrary"),
        name="outproj",
    )(mixed, w_out, x2d, mod4)


def _router_kernel(x_ref, g_ref, sh_ref, sc_ref, wr_ref, o_ref, a_ref):
    d = x_ref.shape[1]
    h = _modulated_norm(x_ref[...], g_ref[...], sh_ref[...], sc_ref[...])
    logits = _dot(h.astype(BF16), wr_ref[...])
    lane = lax.broadcasted_iota(jnp.int32, logits.shape, 1)
    logits = jnp.where(lane < N_EXPERTS, logits, NEG_INF)
    p = jnp.exp(logits - jnp.max(logits, axis=-1, keepdims=True))
    aff = p / jnp.sum(p, axis=-1, keepdims=True)
    o_ref[:, :d] = h
    o_ref[:, d:] = aff
    a_ref[...] = aff[:, :N_EXPERTS]


def _router(x2d, gain, mod4, w_router_pad, rows_per_sample, k_shift):
    r, d = x2d.shape
    tr = 512
    per = rows_per_sample // tr
    return pl.pallas_call(
        _router_kernel,
        out_shape=(jax.ShapeDtypeStruct((r, d + LANES), F32), jax.ShapeDtypeStruct((r, N_EXPERTS), F32)),
        grid=(r // tr,),
        in_specs=[pl.BlockSpec((tr, d), lambda i: (i, 0)),
                  pl.BlockSpec((1, d), lambda i: (0, 0)),
                  pl.BlockSpec((None, None, 1, d), lambda i: (i // per, k_shift, 0, 0)),
                  pl.BlockSpec((None, None, 1, d), lambda i: (i // per, k_shift + 1, 0, 0)),
                  pl.BlockSpec((d, LANES), lambda i: (0, 0))],
        out_specs=(pl.BlockSpec((tr, d + LANES), lambda i: (i, 0)), pl.BlockSpec((tr, N_EXPERTS), lambda i: (i, 0))),
        compiler_params=_params("parallel"),
        name="router",
    )(x2d, gain.reshape(1, d), mod4, mod4, w_router_pad)


TOPK_TILE = 256
GEOMETRIC_STEPS = 32
ARITHMETIC_STEPS = 12


def _topk_kernel(aff_ref, affc_ref, slot_ref, idx_ref, bounds_ref, slot_t_ref, *, cap):
    t_len = aff_ref.shape[0]
    tt = TOPK_TILE
    nt = t_len // tt
    packed = affc_ref[...]

    def per_expert(v):
        shift = LANES // 2
        while shift >= N_EXPERTS:
            v = v + pltpu.roll(v, shift, 1)
            shift //= 2
        return v

    def narrow(c, mid):
        lo, hi = c
        ge = per_expert(jnp.sum(jnp.where(packed >= mid, 1.0, 0.0), axis=0, keepdims=True)) >= cap
        return jnp.where(ge, mid, lo), jnp.where(ge, hi, mid)

    tiny = float(np.finfo(np.float32).tiny)
    above_tiny = per_expert(jnp.sum(jnp.where(packed >= tiny, 1.0, 0.0), axis=0, keepdims=True)) >= cap
    bracket = (jnp.where(above_tiny, tiny, 0.0), jnp.where(above_tiny, 2.0, tiny) + jnp.zeros((1, LANES), F32))
    geometric_mid = lambda c: jnp.clip(jnp.sqrt(c[0]) * jnp.sqrt(c[1]), c[0], c[1])
    bracket = lax.fori_loop(0, GEOMETRIC_STEPS, lambda _, c: narrow(c, geometric_mid(c)), bracket)
    lo, hi = lax.fori_loop(0, ARITHMETIC_STEPS, lambda _, c: narrow(c, 0.5 * (c[0] + c[1])), bracket)

    def count_above(i, cnt):
        r0 = pl.multiple_of(i * tt, tt)
        return cnt + jnp.sum(jnp.where(aff_ref[pl.ds(r0, tt), :] >= hi, 1.0, 0.0), axis=0, keepdims=True)
    need = cap - lax.fori_loop(0, nt, count_above, jnp.zeros((1, LANES), F32))

    tri = jnp.where(lax.broadcasted_iota(jnp.int32, (tt, tt), 0) >= lax.broadcasted_iota(jnp.int32, (tt, tt), 1),
                    1.0, 0.0).astype(BF16)

    def assign(i, carry):
        eq_before, sel_before = carry
        r0 = pl.multiple_of(i * tt, tt)
        a = aff_ref[pl.ds(r0, tt), :]
        above = a >= hi
        tie = (a >= lo) & (a < hi)
        eq = jnp.where(tie, 1.0, 0.0)
        eq_rank = _dot(tri, eq.astype(BF16)) + eq_before
        sel = jnp.where(above | (tie & (eq_rank <= need)), 1.0, 0.0)
        sel_rank = _dot(tri, sel.astype(BF16)) + sel_before
        slot_ref[pl.ds(r0, tt), :] = jnp.where(sel > 0, sel_rank - 1.0, -1.0).astype(jnp.int32)
        bounds_ref[pl.ds(i, 1), :] = sel_before.astype(jnp.int32)
        return (eq_before + jnp.sum(eq, axis=0, keepdims=True), sel_before + jnp.sum(sel, axis=0, keepdims=True))

    zero = jnp.zeros((1, LANES), F32)
    _, total = lax.fori_loop(0, nt, assign, (zero, zero))
    bounds_ref[nt:nt + 1, :] = total.astype(jnp.int32)

    slot_t_ref[...] = jnp.transpose(slot_ref[...].astype(F32))
    idx_ref[...] = jnp.zeros_like(idx_ref)
    sublanes = 8
    tok = lax.broadcasted_iota(jnp.int32, (sublanes, t_len), 1).astype(F32)
    sub = lax.broadcasted_iota(jnp.int32, (sublanes, 1), 0).astype(F32)
    for e in range(N_EXPERTS):
        def body(g, carry):
            s0 = pl.multiple_of(g * sublanes, sublanes)
            hit = slot_t_ref[e:e + 1, :] == sub + s0.astype(F32)
            idx_ref[pl.ds(s0, sublanes), e:e + 1] = jnp.sum(jnp.where(hit, tok, 0.0), axis=1,
                                                            keepdims=True).astype(jnp.int32)
            return carry
        lax.fori_loop(0, cap // sublanes, body, 0, unroll=8)


def _topk(hext, aff, batch, t_len, d, cap):
    nb = t_len // TOPK_TILE + 1
    packed_rows = t_len * N_EXPERTS // LANES
    slot, idx_t, bounds = pl.pallas_call(
        functools.partial(_topk_kernel, cap=cap),
        out_shape=(jax.ShapeDtypeStruct((batch * t_len, LANES), jnp.int32),
                   jax.ShapeDtypeStruct((batch, cap, LANES), jnp.int32),
                   jax.ShapeDtypeStruct((batch, nb, LANES), jnp.int32)),
        grid=(batch,),
        in_specs=[pl.BlockSpec((t_len, LANES), lambda b: (b, d // LANES)),
                  pl.BlockSpec((None, packed_rows, LANES), lambda b: (b, 0, 0))],
        out_specs=(pl.BlockSpec((t_len, LANES), lambda b: (b, 0)),
                   pl.BlockSpec((None, cap, LANES), lambda b: (b, 0, 0)),
                   pl.BlockSpec((None, nb, LANES), lambda b: (b, 0, 0))),
        scratch_shapes=[pltpu.VMEM((LANES, t_len), F32)],
        compiler_params=_params("parallel"),
        name="topk",
    )(hext, aff.reshape(batch, packed_rows, LANES))
    return slot, idx_t[:, :, :N_EXPERTS].transpose(0, 2, 1), bounds


GATHER_ROWS = 256


def _gather_kernel(idx_ref, h_hbm, xe_ref, g_ref, buf, sem, *, t_len, d, nchunk):
    step = pl.program_id(0)
    rc = xe_ref.shape[0]
    cur = step % 2

    def request(st, slot):
        lst, chunk = st // nchunk, st % nchunk
        row0 = (lst // N_EXPERTS) * t_len

        def issue(r, carry):
            row = row0 + idx_ref[lst, chunk * rc + r]
            pltpu.make_async_copy(h_hbm.at[pl.ds(row, 1)], buf.at[slot, pl.ds(r, 1)], sem.at[slot]).start()
            return carry

        lax.fori_loop(0, rc, issue, 0, unroll=8)

    @pl.when(step == 0)
    def _():
        request(0, 0)

    @pl.when(step + 1 < pl.num_programs(0))
    def _():
        request(step + 1, 1 - cur)

    pltpu.make_async_copy(h_hbm.at[pl.ds(0, rc)], buf.at[cur], sem.at[cur]).wait()
    rows = buf[cur]
    xe_ref[...] = rows[:, :d].astype(xe_ref.dtype)
    aff = rows[:, d:]
    e = (step // nchunk) % N_EXPERTS
    lane = lax.broadcasted_iota(jnp.int32, aff.shape, 1)
    g_ref[...] = jnp.broadcast_to(jnp.sum(jnp.where(lane == e, aff, 0.0), axis=1, keepdims=True), aff.shape)


def _gather(idx, hext, batch, t_len, d, cap):
    rc = min(GATHER_ROWS, cap)
    nchunk = cap // rc

    def out_block(s, idx):
        lst = s // nchunk
        return lst % N_EXPERTS, (lst // N_EXPERTS) * nchunk + s % nchunk, 0

    return pl.pallas_call(
        functools.partial(_gather_kernel, t_len=t_len, d=d, nchunk=nchunk),
        out_shape=(jax.ShapeDtypeStruct((N_EXPERTS, batch * cap, d), BF16),
                   jax.ShapeDtypeStruct((N_EXPERTS, batch * cap, LANES), F32)),
        grid_spec=pltpu.PrefetchScalarGridSpec(
            num_scalar_prefetch=1,
            grid=(batch * N_EXPERTS * nchunk,),
            in_specs=[pl.BlockSpec(memory_space=pl.ANY)],
            out_specs=(pl.BlockSpec((None, rc, d), out_block), pl.BlockSpec((None, rc, LANES), out_block)),
            scratch_shapes=[pltpu.VMEM((2, rc, d + LANES), F32), pltpu.SemaphoreType.DMA((2,))]),
        compiler_params=_params("arbitrary"),
        name="gather",
    )(idx.reshape(batch * N_EXPERTS, cap), hext)


def _expert_up_kernel(x_ref, wg_ref, wu_ref, o_ref):
    x = x_ref[...]
    a = _dot(x, wg_ref[...].astype(BF16))
    u = _dot(x, wu_ref[...].astype(BF16))
    o_ref[...] = (_silu(a) * u).astype(o_ref.dtype)


def _expert_up(xe, w_gate, w_up):
    e, m, d = xe.shape
    ff = w_gate.shape[2]
    tf = 256
    return pl.pallas_call(
        _expert_up_kernel,
        out_shape=jax.ShapeDtypeStruct((e, m, ff), BF16),
        grid=(e, ff // tf),
        in_specs=[pl.BlockSpec((None, m, d), lambda i, f: (i, 0, 0)),
                  pl.BlockSpec((None, d, tf), lambda i, f: (i, 0, f)),
                  pl.BlockSpec((None, d, tf), lambda i, f: (i, 0, f))],
        out_specs=pl.BlockSpec((None, m, tf), lambda i, f: (i, 0, f)),
        compiler_params=_params("parallel", "arbitrary"),
        name="expert_up",
    )(xe, w_gate, w_up)


def _expert_down_kernel(a_ref, w_ref, g_ref, o_ref):
    o_ref[...] = (_dot(a_ref[...], w_ref[...].astype(BF16)) * g_ref[:, :1]).astype(o_ref.dtype)


def _expert_down(act, w_down, g):
    e, m, ff = act.shape
    d = w_down.shape[2]
    tn = min(1024, d)
    return pl.pallas_call(
        _expert_down_kernel,
        out_shape=jax.ShapeDtypeStruct((e, m, d), BF16),
        grid=(e, d // tn),
        in_specs=[pl.BlockSpec((None, m, ff), lambda i, j: (i, 0, 0)),
                  pl.BlockSpec((None, ff, tn), lambda i, j: (i, 0, j)),
                  pl.BlockSpec((None, m, LANES), lambda i, j: (i, 0, 0))],
        out_specs=pl.BlockSpec((None, m, tn), lambda i, j: (i, 0, j)),
        compiler_params=_params("parallel", "arbitrary"),
        name="expert_down",
    )(act, w_down, g)


COMBINE_WINDOW = 64
ROW_ALIGN = 16


def _combine_kernel(bounds_ref, slot_ref, ye_hbm, x_ref, g_ref, fn_ref, o_ref, stage, onehot, sem, *, nt, cap):
    step = pl.program_id(0)
    w = COMBINE_WINDOW
    cur = step % 2

    def tile_rows(st, e):
        b, i = st // nt, st % nt
        return b, bounds_ref[b * (nt + 1) + i, e], bounds_ref[b * (nt + 1) + i + 1, e]

    def window(st, e, r):
        b, first, _ = tile_rows(st, e)
        base = (first // ROW_ALIGN) * ROW_ALIGN + r * w
        return b, base, jnp.minimum(base, cap - w)

    def window_copy(st, e, r, buf):
        b, _, src = window(st, e, r)
        return pltpu.make_async_copy(ye_hbm.at[e, pl.ds(pl.multiple_of(b * cap + src, ROW_ALIGN), w), :],
                                     stage.at[buf, pl.ds(e * w, w), :], sem.at[buf])

    def start_round(st, r, buf):
        for e in range(N_EXPERTS):
            window_copy(st, e, r, buf).start()

    def wait_round(st, r, buf):
        for e in range(N_EXPERTS):
            window_copy(st, e, r, buf).wait()

    def scatter(r):
        slots = slot_ref[...]
        pos = lax.broadcasted_iota(jnp.int32, (1, w), 1)
        for e in range(N_EXPERTS):
            _, base, src = window(step, e, r)
            col = slots[:, e:e + 1]
            hit = (col >= base) & (col - src == pos)
            onehot[:, e * w:(e + 1) * w] = jnp.where(hit, 1.0, 0.0).astype(BF16)
        return _dot(onehot[...], stage[cur])

    @pl.when(step == 0)
    def _():
        start_round(0, 0, 0)

    @pl.when(step + 1 < pl.num_programs(0))
    def _():
        start_round(step + 1, 0, 1 - cur)

    wait_round(step, 0, cur)
    o_ref[...] = scatter(0)

    rounds = 1
    for e in range(N_EXPERTS):
        _, first, last = tile_rows(step, e)
        rounds = jnp.maximum(rounds, (last - (first // ROW_ALIGN) * ROW_ALIGN + w - 1) // w)

    def extra_round(r, carry):
        start_round(step, r, cur)
        wait_round(step, r, cur)
        o_ref[...] += scatter(r)
        return carry

    lax.fori_loop(1, rounds, extra_round, 0)

    v = x_ref[...] + g_ref[...] * o_ref[...]
    y = v * lax.rsqrt(jnp.mean(v * v, axis=-1, keepdims=True) + NORM_EPS)
    o_ref[...] = y * fn_ref[...]


def _combine(bounds, slot, ye, x2d, mod4, final_norm, batch, t_len, cap, k_gate):
    m, d = x2d.shape
    tm = TOPK_TILE
    nt = t_len // tm
    w = COMBINE_WINDOW
    return pl.pallas_call(
        functools.partial(_combine_kernel, nt=nt, cap=cap),
        out_shape=jax.ShapeDtypeStruct((m, d), F32),
        grid_spec=pltpu.PrefetchScalarGridSpec(
            num_scalar_prefetch=1,
            grid=(batch * nt,),
            in_specs=[pl.BlockSpec((tm, LANES), lambda s, bnd: (s, 0)),
                      pl.BlockSpec(memory_space=pl.ANY),
                      pl.BlockSpec((tm, d), lambda s, bnd: (s, 0)),
                      pl.BlockSpec((None, None, 1, d), lambda s, bnd: (s // nt, k_gate, 0, 0)),
                      pl.BlockSpec((1, d), lambda s, bnd: (0, 0))],
            out_specs=pl.BlockSpec((tm, d), lambda s, bnd: (s, 0)),
            scratch_shapes=[pltpu.VMEM((2, N_EXPERTS * w, d), BF16), pltpu.VMEM((tm, N_EXPERTS * w), BF16),
                            pltpu.SemaphoreType.DMA((2,))]),
        compiler_params=_params("arbitrary"),
        name="combine",
    )(bounds.reshape(batch * (nt + 1), LANES), slot, ye, x2d, mod4, final_norm.reshape(1, d))


def kernel(x, c, ctx, c_ctx, norm1, norm2, w_mod, b_mod, w_in, na_rpb, ret_decay, w_branch_na, w_branch_ret,
           w_out, w_router, w_gate, w_up, w_down, final_norm):
    batch, t_len, d = x.shape
    tc_len = ctx.shape[1]
    na_w = NA_HEADS * NA_HEAD_DIM
    ret_w = RET_HEADS * RET_DIM
    col_qa, col_ka, col_va = 0, na_w, 2 * na_w
    col_qr = 3 * na_w
    col_kr, col_vr, col_gf, col_gb = col_qr + ret_w, col_qr + 2 * ret_w, col_qr + 3 * ret_w, col_qr + 4 * ret_w
    col_ga = col_qr + 5 * ret_w
    col_gb2 = col_ga + d
    cap = EC_CAPACITY_FACTOR * t_len // N_EXPERTS
    assert w_in.shape[0] == 1, "single layer"

    x2d = x.reshape(batch * t_len, d)
    cvec = jnp.concatenate([c, c_ctx[None], jnp.zeros((8 - batch - 1, d), F32)], axis=0)
    mod = _modulation(cvec, w_mod[0], b_mod[0])
    mod4 = mod[:batch + 1].reshape(batch + 1, N_MOD, 1, d)

    h = _prenorm(x2d, norm1[0], mod4, t_len, 0, 0)
    hc = _prenorm(ctx.reshape(batch * tc_len, d), norm1[0], mod4, batch * tc_len, batch, 0)
    tn = 512
    proj = _matmul(h, w_in[0], w_in.shape[2], 2048, tn, lambda j: j, "in_proj")
    kv_w = 2 * na_w
    cproj = _matmul(hc, w_in[0], kv_w + 2 * ret_w, batch * tc_len, tn,
                    lambda j: jnp.where(j < kv_w // tn, col_ka // tn + j, col_kr // tn + j - kv_w // tn), "ctx_proj")

    y_na = _na_attention(proj, cproj, na_rpb[0], batch, t_len, tc_len, col_qa, col_ka, col_va)
    y_ret = _retention(proj, cproj, ret_decay[0], batch, t_len, tc_len, col_qr, col_kr, col_vr, col_gf, col_gb,
                       kv_w, kv_w + ret_w)
    mixed = _merge(y_na, y_ret, w_branch_na[0], w_branch_ret[0], proj, col_ga, col_gb2)
    x1 = _outproj(mixed, w_out[0], x2d, mod4, t_len, 2)

    w_router_pad = jnp.pad(w_router[0], ((0, 0), (0, LANES - N_EXPERTS))).astype(BF16)
    hext, aff = _router(x1, norm2[0], mod4, w_router_pad, t_len, 3)
    slot, idx, bounds = _topk(hext, aff, batch, t_len, d, cap)
    xe, g = _gather(idx, hext, batch, t_len, d, cap)
    act = _expert_up(xe, w_gate[0], w_up[0])
    ye = _expert_down(act, w_down[0], g)
    out = _combine(bounds, slot, ye, x1, mod4, final_norm, batch, t_len, cap, 5)
    return out.reshape(batch, t_len, d)
```

```python
import functools

import numpy as np
import jax
import jax.numpy as jnp
from jax import lax
from jax.experimental import pallas as pl
from jax.experimental.pallas import tpu as pltpu

F32 = jnp.float32
BF16 = jnp.bfloat16

GRID_W = 64
NA_HEADS = 16
NA_HEAD_DIM = 128
NA_WIN_R = 8
NA_WIN_C = 16
RET_HEADS = 8
RET_DIM = 256
RET_CHUNK = 256
N_EXPERTS = 16
EC_CAPACITY_FACTOR = 2
ROPE_BASE = 10000.0
NORM_EPS = 1e-6
NEG_INF = -1e30
N_MOD = 6

VMEM_LIMIT_BYTES = 56 * 1024 * 1024
LANES = 128

NA_Q_ROWS = 4
NA_K_ROWS = NA_Q_ROWS + NA_WIN_R


def _params(*sem):
    return pltpu.CompilerParams(dimension_semantics=sem, vmem_limit_bytes=VMEM_LIMIT_BYTES)


def _dot(a, b):
    return jnp.dot(a, b, preferred_element_type=F32)


def _dot_nt(a, b):
    return lax.dot_general(a, b, (((1,), (1,)), ((), ())), preferred_element_type=F32)


def _dot_tn(a, b):
    return lax.dot_general(a, b, (((0,), (0,)), ((), ())), preferred_element_type=F32)


def _silu(x):
    return x * jax.nn.sigmoid(x)


def _mod_kernel(c_ref, w_ref, b_ref, o_ref):
    a = _silu(c_ref[...]).astype(BF16)
    o_ref[...] = _dot(a, w_ref[...].astype(BF16)) + b_ref[...]


def _modulation(cvec, w_mod, b_mod):
    r, d = cvec.shape
    n = w_mod.shape[1]
    tn = 1024
    return pl.pallas_call(
        _mod_kernel,
        out_shape=jax.ShapeDtypeStruct((r, n), F32),
        grid=(n // tn,),
        in_specs=[pl.BlockSpec((r, d), lambda j: (0, 0)),
                  pl.BlockSpec((d, tn), lambda j: (0, j)),
                  pl.BlockSpec((1, tn), lambda j: (0, j))],
        out_specs=pl.BlockSpec((r, tn), lambda j: (0, j)),
        compiler_params=_params("arbitrary"),
        name="modulation",
    )(cvec, w_mod, b_mod.reshape(1, n))


def _modulated_norm(x, g, shift, scale):
    y = x * lax.rsqrt(jnp.mean(x * x, axis=-1, keepdims=True) + NORM_EPS)
    return (y * g) * (1.0 + scale) + shift


def _prenorm_kernel(x_ref, g_ref, sh_ref, sc_ref, o_ref):
    o_ref[...] = _modulated_norm(x_ref[...], g_ref[...], sh_ref[...], sc_ref[...]).astype(o_ref.dtype)


def _prenorm(x2d, gain, mod4, rows_per_sample, sample0, k_shift):
    r, d = x2d.shape
    tr = 512
    per = rows_per_sample // tr
    return pl.pallas_call(
        _prenorm_kernel,
        out_shape=jax.ShapeDtypeStruct((r, d), BF16),
        grid=(r // tr,),
        in_specs=[pl.BlockSpec((tr, d), lambda i: (i, 0)),
                  pl.BlockSpec((1, d), lambda i: (0, 0)),
                  pl.BlockSpec((None, None, 1, d), lambda i: (sample0 + i // per, k_shift, 0, 0)),
                  pl.BlockSpec((None, None, 1, d), lambda i: (sample0 + i // per, k_shift + 1, 0, 0))],
        out_specs=pl.BlockSpec((tr, d), lambda i: (i, 0)),
        compiler_params=_params("parallel"),
        name="prenorm",
    )(x2d, gain.reshape(1, d), mod4, mod4)


def _mm_kernel(a_ref, w_ref, o_ref):
    o_ref[...] = _dot(a_ref[...], w_ref[...].astype(BF16)).astype(o_ref.dtype)


def _matmul(a, w, n_out, tm, tn, col_block, name):
    m, k = a.shape
    return pl.pallas_call(
        _mm_kernel,
        out_shape=jax.ShapeDtypeStruct((m, n_out), BF16),
        grid=(m // tm, n_out // tn),
        in_specs=[pl.BlockSpec((tm, k), lambda i, j: (i, 0), pipeline_mode=pl.Buffered(1)),
                  pl.BlockSpec((k, tn), lambda i, j: (0, col_block(j)))],
        out_specs=pl.BlockSpec((tm, tn), lambda i, j: (i, j)),
        compiler_params=_params("parallel", "arbitrary"),
        name=name,
    )(a, w)


def _na_tables(rows):
    wr = NA_WIN_R
    bases, tables = [], []
    for t in range(rows // NA_Q_ROWS):
        kb = int(np.clip(NA_Q_ROWS * t - wr // 2, 0, rows - NA_K_ROWS))
        tab = []
        for i in range(NA_Q_ROWS):
            r = NA_Q_ROWS * t + i
            r0 = int(np.clip(r - wr // 2, 0, rows - wr))
            tab.append(tuple((kb + j - r + NA_WIN_R - 1) if r0 <= kb + j < r0 + wr else None
                             for j in range(NA_K_ROWS)))
        bases.append(kb)
        tables.append(tuple(tab))
    uniq = list(dict.fromkeys(tables))
    return bases, uniq, [uniq.index(t) for t in tables]


def _na_bias_diagonals(rpb):
    offset = np.clip(np.arange(LANES) - (GRID_W - 1), -(NA_WIN_C - 1), NA_WIN_C - 1) + (NA_WIN_C - 1)
    return rpb.astype(F32)[:, :, offset]


def _na_build_bias(diag_ref, rowbias_ref, bias_ref, patterns):
    assert LANES == 2 * GRID_W
    shape = (GRID_W, LANES)
    qc = lax.broadcasted_iota(jnp.int32, shape, 0)
    lane = lax.broadcasted_iota(jnp.int32, shape, 1)
    kc = lane % GRID_W
    c0 = jnp.clip(qc - NA_WIN_C // 2, 0, GRID_W - NA_WIN_C)
    col_ok = (kc >= c0) & (kc < c0 + NA_WIN_C)
    low = lane < GRID_W
    for dr in range(diag_ref.shape[0]):
        diag = jnp.broadcast_to(diag_ref[dr:dr + 1, :], shape)
        first = pltpu.roll(diag, GRID_W + 1, 1, stride=1, stride_axis=0)
        second = pltpu.roll(diag, 1, 1, stride=1, stride_axis=0)
        rowbias_ref[dr] = jnp.where(col_ok, jnp.where(low, first, second), NEG_INF)
    outside = jnp.full(shape, NEG_INF, F32)
    block = lambda dr: outside if dr is None else rowbias_ref[dr]
    for p, tab in enumerate(patterns):
        for i, row in enumerate(tab):
            for j in range(0, NA_K_ROWS, 2):
                bias_ref[p, i * GRID_W:(i + 1) * GRID_W, j * GRID_W:(j + 2) * GRID_W] = jnp.where(
                    low, block(row[j]), block(row[j + 1]))


def _na_kernel(q_ref, k_ref, v_ref, kc_ref, vc_ref, diag_ref, o_ref, rowbias_ref, bias_ref, vext_ref, vcext_ref, *,
               rows, patterns, pat_ids):
    tq = NA_Q_ROWS * GRID_W
    nk = NA_K_ROWS * GRID_W
    scale = NA_HEAD_DIM ** -0.5
    for src, dst in ((v_ref, vext_ref), (vc_ref, vcext_ref)):
        dst[:, :NA_HEAD_DIM] = src[...]
        dst[:, NA_HEAD_DIM:] = jnp.ones_like(src)

    @pl.when(pl.program_id(1) == 0)
    def _():
        _na_build_bias(diag_ref, rowbias_ref, bias_ref, patterns)

    def tile(t, carry):
        kb = pl.multiple_of(jnp.clip(NA_Q_ROWS * t - NA_WIN_R // 2, 0, rows - NA_K_ROWS) * GRID_W, tq)
        q0 = pl.multiple_of(t * tq, tq)
        pat = 0
        for i, pid in enumerate(pat_ids):
            pat = jnp.where(t == i, pid, pat)
        q = q_ref[pl.ds(q0, tq), :]
        s_w = _dot_nt(q, k_ref[pl.ds(kb, nk), :]) * scale + bias_ref[pat]
        s_c = _dot_nt(q, kc_ref[...]) * scale
        m = jnp.maximum(jnp.max(s_w, axis=-1, keepdims=True), jnp.max(s_c, axis=-1, keepdims=True))
        p_w = jnp.exp(s_w - m)
        p_c = jnp.exp(s_c - m)
        o = _dot(p_w.astype(BF16), vext_ref[pl.ds(kb, nk), :]) + _dot(p_c.astype(BF16), vcext_ref[...])
        dh = NA_HEAD_DIM
        o_ref[pl.ds(q0, tq), :] = (o[:, :dh] / o[:, dh:dh + 1]).astype(o_ref.dtype)
        return carry

    lax.fori_loop(0, rows // NA_Q_ROWS, tile, 0, unroll=4)


def _na_attention(proj, cproj, rpb, batch, t_len, tc_len, col_q, col_k, col_v):
    rows = t_len // GRID_W
    tq = NA_Q_ROWS * GRID_W
    _, patterns, pat_ids = _na_tables(rows)
    diag = _na_bias_diagonals(rpb)
    n_dr = diag.shape[1]
    dh = NA_HEAD_DIM
    tok = lambda col: pl.BlockSpec((t_len, dh), lambda h, b: (b, col // dh + h))
    return pl.pallas_call(
        functools.partial(_na_kernel, rows=rows, patterns=tuple(patterns), pat_ids=tuple(pat_ids)),
        out_shape=jax.ShapeDtypeStruct((batch * t_len, NA_HEADS * dh), BF16),
        grid=(NA_HEADS, batch),
        in_specs=[tok(col_q), tok(col_k), tok(col_v),
                  pl.BlockSpec((tc_len, dh), lambda h, b: (b, h)),
                  pl.BlockSpec((tc_len, dh), lambda h, b: (b, NA_HEADS + h)),
                  pl.BlockSpec((None, n_dr, LANES), lambda h, b: (h, 0, 0))],
        out_specs=pl.BlockSpec((t_len, dh), lambda h, b: (b, h)),
        scratch_shapes=[pltpu.VMEM((n_dr, GRID_W, LANES), F32),
                        pltpu.VMEM((len(patterns), tq, NA_K_ROWS * GRID_W), F32),
                        pltpu.VMEM((t_len, 2 * dh), BF16), pltpu.VMEM((tc_len, 2 * dh), BF16)],
        compiler_params=_params("arbitrary", "arbitrary"),
        name="na_attention",
    )(proj, proj, proj, cproj, cproj, diag)


def _rope_tables(t_len):
    quarter = RET_DIM // 4
    inv = ROPE_BASE ** (-np.arange(quarter, dtype=np.float64) / quarter)
    tpos = np.arange(t_len)
    row_ang = (tpos // GRID_W).astype(np.float64)[:, None] * inv[None, :]
    col_ang = (tpos % GRID_W).astype(np.float64)[:, None] * inv[None, :]
    cos = np.concatenate([np.cos(row_ang)] * 2 + [np.cos(col_ang)] * 2, axis=-1)
    sin = np.concatenate([-np.sin(row_ang), np.sin(row_ang), -np.sin(col_ang), np.sin(col_ang)], axis=-1)
    return jnp.asarray(cos, F32), jnp.asarray(sin, F32)


def _rope(a, cos, sin):
    half = RET_DIM // 2
    swapped = jnp.concatenate([pltpu.roll(a[:, :half], half // 2, 1), pltpu.roll(a[:, half:], half // 2, 1)], axis=1)
    return a * cos + swapped * sin


def _ret_kernel(dec_ref, q_ref, k_ref, v_ref, gf_ref, gb_ref, kc_ref, vc_ref, cos_ref, sin_ref, o_ref,
                sf_ref, sb_ref, acc_ref, qrot_ref, krot_ref):
    h = pl.program_id(1)
    c = RET_CHUNK
    t_len = q_ref.shape[0]
    tc_len = kc_ref.shape[0]
    nc = t_len // c
    k_scale = RET_DIM ** -0.5

    def log_gamma(direction):
        e = jnp.full((1, 1), dec_ref[direction, h], F32)
        return jnp.log1p(-jnp.exp2(-e))

    lg_f, lg_b = log_gamma(0), log_gamma(1)
    pos = lax.broadcasted_iota(jnp.int32, (c, 1), 0).astype(F32)
    diff = pos - lax.broadcasted_iota(jnp.int32, (1, c), 1).astype(F32)
    dec_f = jnp.where(diff >= 0, jnp.exp(lg_f * jnp.maximum(diff, 0.0)), 0.0) * k_scale
    dec_b = jnp.where(diff <= 0, jnp.exp(lg_b * jnp.maximum(-diff, 0.0)), 0.0) * k_scale
    qdec_f, kdec_f, cdec_f = jnp.exp(lg_f * (pos + 1.0)), jnp.exp(lg_f * (c - 1.0 - pos)) * k_scale, jnp.exp(lg_f * c)
    qdec_b, kdec_b, cdec_b = jnp.exp(lg_b * (c - pos)), jnp.exp(lg_b * pos) * k_scale, jnp.exp(lg_b * c)

    cpos = lax.broadcasted_iota(jnp.int32, (tc_len, 1), 0).astype(F32)
    kc = kc_ref[...].astype(F32) * k_scale
    vc = vc_ref[...]
    sf_ref[...] = _dot_tn((kc * jnp.exp(lg_f * (tc_len - 1.0 - cpos))).astype(BF16), vc)
    sb_ref[...] = _dot_tn((kc * jnp.exp(lg_b * cpos)).astype(BF16), vc)

    def chunk(n, s_ref, dec, qdec, kdec, cdec, g_ref, first_visit):
        rows = pl.ds(pl.multiple_of(n * c, c), c)
        if first_visit:
            cos, sin = cos_ref[rows, :], sin_ref[rows, :]
            qb = _rope(q_ref[rows, :].astype(F32), cos, sin).astype(BF16)
            k = _rope(k_ref[rows, :].astype(F32), cos, sin)
            qrot_ref[rows, :] = qb
            krot_ref[rows, :] = k
        else:
            qb, k = qrot_ref[rows, :], krot_ref[rows, :]
        v = v_ref[rows, :]
        scores = _dot_nt(qb, k.astype(BF16)) * dec
        s = s_ref[...]
        o = _dot(scores.astype(BF16), v) + _dot(qb, s.astype(BF16)) * qdec
        s_ref[...] = s * cdec + _dot_tn((k * kdec).astype(BF16), v)
        on = o * lax.rsqrt(jnp.mean(o * o, axis=-1, keepdims=True) + NORM_EPS)
        gated = _silu(g_ref[rows, :].astype(F32)) * on
        if first_visit:
            acc_ref[rows, :] = gated
        else:
            acc_ref[rows, :] += gated

    assert nc % 2 == 0

    def steps(first_visit):
        def body(n, carry):
            chunk(n, sf_ref, dec_f, qdec_f, kdec_f, cdec_f, gf_ref, first_visit)
            chunk(nc - 1 - n, sb_ref, dec_b, qdec_b, kdec_b, cdec_b, gb_ref, first_visit)
            return carry
        return body

    lax.fori_loop(0, nc // 2, steps(True), 0, unroll=2)
    lax.fori_loop(nc // 2, nc, steps(False), 0, unroll=4)
    o_ref[...] = acc_ref[...].astype(o_ref.dtype)


def _retention(proj, cproj, ret_decay, batch, t_len, tc_len, col_q, col_k, col_v, col_gf, col_gb, ccol_k, ccol_v):
    d = RET_DIM
    cos, sin = _rope_tables(t_len)
    tok = lambda col: pl.BlockSpec((t_len, d), lambda b, h: (b, col // d + h))
    ctx = lambda col: pl.BlockSpec((tc_len, d), lambda b, h: (b, col // d + h))
    tab = pl.BlockSpec((t_len, d), lambda b, h: (0, 0))
    return pl.pallas_call(
        _ret_kernel,
        out_shape=jax.ShapeDtypeStruct((batch * t_len, RET_HEADS * d), BF16),
        grid=(batch, RET_HEADS),
        in_specs=[pl.BlockSpec(memory_space=pltpu.SMEM),
                  tok(col_q), tok(col_k), tok(col_v), tok(col_gf), tok(col_gb), ctx(ccol_k), ctx(ccol_v), tab, tab],
        out_specs=pl.BlockSpec((t_len, d), lambda b, h: (b, h)),
        scratch_shapes=[pltpu.VMEM((d, d), F32), pltpu.VMEM((d, d), F32), pltpu.VMEM((t_len, d), F32),
                        pltpu.VMEM((t_len, d), BF16), pltpu.VMEM((t_len, d), F32)],
        compiler_params=_params("parallel", "arbitrary"),
        name="retention",
    )(ret_decay.astype(F32), proj, proj, proj, proj, proj, cproj, cproj, cos, sin)


def _merge_kernel(ya_ref, yr_ref, wa_ref, wr_ref, ga_ref, gb_ref, o_ref):
    a = _dot(ya_ref[...], wa_ref[...].astype(BF16))
    r = _dot(yr_ref[...], wr_ref[...].astype(BF16))
    o_ref[...] = (jax.nn.sigmoid(ga_ref[...].astype(F32)) * a + jax.nn.sigmoid(gb_ref[...].astype(F32)) * r
                  ).astype(o_ref.dtype)


def _merge(y_na, y_ret, w_na, w_ret, proj, col_ga, col_gb):
    m, ka = y_na.shape
    kr = y_ret.shape[1]
    n = w_na.shape[1]
    tm, tn = 1024, 512
    return pl.pallas_call(
        _merge_kernel,
        out_shape=jax.ShapeDtypeStruct((m, n), BF16),
        grid=(m // tm, n // tn),
        in_specs=[pl.BlockSpec((tm, ka), lambda i, j: (i, 0)),
                  pl.BlockSpec((tm, kr), lambda i, j: (i, 0)),
                  pl.BlockSpec((ka, tn), lambda i, j: (0, j)),
                  pl.BlockSpec((kr, tn), lambda i, j: (0, j)),
                  pl.BlockSpec((tm, tn), lambda i, j: (i, col_ga // tn + j)),
                  pl.BlockSpec((tm, tn), lambda i, j: (i, col_gb // tn + j))],
        out_specs=pl.BlockSpec((tm, tn), lambda i, j: (i, j)),
        compiler_params=_params("parallel", "arbitrary"),
        name="merge",
    )(y_na, y_ret, w_na, w_ret, proj, proj)


def _outproj_kernel(m_ref, w_ref, x_ref, g_ref, o_ref):
    o_ref[...] = x_ref[...] + g_ref[...] * _dot(m_ref[...], w_ref[...].astype(BF16))


def _outproj(mixed, w_out, x2d, mod4, rows_per_sample, k_gate):
    m, k = mixed.shape
    n = w_out.shape[1]
    tm, tn = 1024, 512
    per = rows_per_sample // tm
    return pl.pallas_call(
        _outproj_kernel,
        out_shape=jax.ShapeDtypeStruct((m, n), F32),
        grid=(m // tm, n // tn),
        in_specs=[pl.BlockSpec((tm, k), lambda i, j: (i, 0)),
                  pl.BlockSpec((k, tn), lambda i, j: (0, j)),
                  pl.BlockSpec((tm, tn), lambda i, j: (i, j)),
                  pl.BlockSpec((None, None, 1, tn), lambda i, j: (i // per, k_gate, 0, j))],
        out_specs=pl.BlockSpec((tm, tn), lambda i, j: (i, j)),
        compiler_params=_params("parallel", "arbitrary"),
        name="outproj",
    )(mixed, w_out, x2d, mod4)


def _router_kernel(x_ref, g_ref, sh_ref, sc_ref, wr_ref, o_ref, a_ref):
    d = x_ref.shape[1]
    h = _modulated_norm(x_ref[...], g_ref[...], sh_ref[...], sc_ref[...])
    logits = _dot(h.astype(BF16), wr_ref[...])
    lane = lax.broadcasted_iota(jnp.int32, logits.shape, 1)
    logits = jnp.where(lane < N_EXPERTS, logits, NEG_INF)
    p = jnp.exp(logits - jnp.max(logits, axis=-1, keepdims=True))
    aff = p / jnp.sum(p, axis=-1, keepdims=True)
    o_ref[:, :d] = h
    o_ref[:, d:] = aff
    a_ref[...] = aff[:, :N_EXPERTS]


def _router(x2d, gain, mod4, w_router_pad, rows_per_sample, k_shift):
    r, d = x2d.shape
    tr = 512
    per = rows_per_sample // tr
    return pl.pallas_call(
        _router_kernel,
        out_shape=(jax.ShapeDtypeStruct((r, d + LANES), F32), jax.ShapeDtypeStruct((r, N_EXPERTS), F32)),
        grid=(r // tr,),
        in_specs=[pl.BlockSpec((tr, d), lambda i: (i, 0)),
                  pl.BlockSpec((1, d), lambda i: (0, 0)),
                  pl.BlockSpec((None, None, 1, d), lambda i: (i // per, k_shift, 0, 0)),
                  pl.BlockSpec((None, None, 1, d), lambda i: (i // per, k_shift + 1, 0, 0)),
                  pl.BlockSpec((d, LANES), lambda i: (0, 0))],
        out_specs=(pl.BlockSpec((tr, d + LANES), lambda i: (i, 0)), pl.BlockSpec((tr, N_EXPERTS), lambda i: (i, 0))),
        compiler_params=_params("parallel"),
        name="router",
    )(x2d, gain.reshape(1, d), mod4, mod4, w_router_pad)


TOPK_TILE = 256
GEOMETRIC_STEPS = 32
ARITHMETIC_STEPS = 12


def _topk_kernel(aff_ref, affc_ref, slot_ref, idx_ref, bounds_ref, slot_t_ref, *, cap):
    t_len = aff_ref.shape[0]
    tt = TOPK_TILE
    nt = t_len // tt
    packed = affc_ref[...]

    def per_expert(v):
        shift = LANES // 2
        while shift >= N_EXPERTS:
            v = v + pltpu.roll(v, shift, 1)
            shift //= 2
        return v

    def narrow(c, mid):
        lo, hi = c
        ge = per_expert(jnp.sum(jnp.where(packed >= mid, 1.0, 0.0), axis=0, keepdims=True)) >= cap
        return jnp.where(ge, mid, lo), jnp.where(ge, hi, mid)

    tiny = float(np.finfo(np.float32).tiny)
    above_tiny = per_expert(jnp.sum(jnp.where(packed >= tiny, 1.0, 0.0), axis=0, keepdims=True)) >= cap
    bracket = (jnp.where(above_tiny, tiny, 0.0), jnp.where(above_tiny, 2.0, tiny) + jnp.zeros((1, LANES), F32))
    geometric_mid = lambda c: jnp.clip(jnp.sqrt(c[0]) * jnp.sqrt(c[1]), c[0], c[1])
    bracket = lax.fori_loop(0, GEOMETRIC_STEPS, lambda _, c: narrow(c, geometric_mid(c)), bracket)
    lo, hi = lax.fori_loop(0, ARITHMETIC_STEPS, lambda _, c: narrow(c, 0.5 * (c[0] + c[1])), bracket)

    def count_above(i, cnt):
        r0 = pl.multiple_of(i * tt, tt)
        return cnt + jnp.sum(jnp.where(aff_ref[pl.ds(r0, tt), :] >= hi, 1.0, 0.0), axis=0, keepdims=True)
    need = cap - lax.fori_loop(0, nt, count_above, jnp.zeros((1, LANES), F32))

    tri = jnp.where(lax.broadcasted_iota(jnp.int32, (tt, tt), 0) >= lax.broadcasted_iota(jnp.int32, (tt, tt), 1),
                    1.0, 0.0).astype(BF16)

    def assign(i, carry):
        eq_before, sel_before = carry
        r0 = pl.multiple_of(i * tt, tt)
        a = aff_ref[pl.ds(r0, tt), :]
        above = a >= hi
        tie = (a >= lo) & (a < hi)
        eq = jnp.where(tie, 1.0, 0.0)
        eq_rank = _dot(tri, eq.astype(BF16)) + eq_before
        sel = jnp.where(above | (tie & (eq_rank <= need)), 1.0, 0.0)
        sel_rank = _dot(tri, sel.astype(BF16)) + sel_before
        slot_ref[pl.ds(r0, tt), :] = jnp.where(sel > 0, sel_rank - 1.0, -1.0).astype(jnp.int32)
        bounds_ref[pl.ds(i, 1), :] = sel_before.astype(jnp.int32)
        return (eq_before + jnp.sum(eq, axis=0, keepdims=True), sel_before + jnp.sum(sel, axis=0, keepdims=True))

    zero = jnp.zeros((1, LANES), F32)
    _, total = lax.fori_loop(0, nt, assign, (zero, zero))
    bounds_ref[nt:nt + 1, :] = total.astype(jnp.int32)

    slot_t_ref[...] = jnp.transpose(slot_ref[...].astype(F32))
    idx_ref[...] = jnp.zeros_like(idx_ref)
    sublanes = 8
    tok = lax.broadcasted_iota(jnp.int32, (sublanes, t_len), 1).astype(F32)
    sub = lax.broadcasted_iota(jnp.int32, (sublanes, 1), 0).astype(F32)
    for e in range(N_EXPERTS):
        def body(g, carry):
            s0 = pl.multiple_of(g * sublanes, sublanes)
            hit = slot_t_ref[e:e + 1, :] == sub + s0.astype(F32)
            idx_ref[pl.ds(s0, sublanes), e:e + 1] = jnp.sum(jnp.where(hit, tok, 0.0), axis=1,
                                                            keepdims=True).astype(jnp.int32)
            return carry
        lax.fori_loop(0, cap // sublanes, body, 0, unroll=8)


def _topk(hext, aff, batch, t_len, d, cap):
    nb = t_len // TOPK_TILE + 1
    packed_rows = t_len * N_EXPERTS // LANES
    slot, idx_t, bounds = pl.pallas_call(
        functools.partial(_topk_kernel, cap=cap),
        out_shape=(jax.ShapeDtypeStruct((batch * t_len, LANES), jnp.int32),
                   jax.ShapeDtypeStruct((batch, cap, LANES), jnp.int32),
                   jax.ShapeDtypeStruct((batch, nb, LANES), jnp.int32)),
        grid=(batch,),
        in_specs=[pl.BlockSpec((t_len, LANES), lambda b: (b, d // LANES)),
                  pl.BlockSpec((None, packed_rows, LANES), lambda b: (b, 0, 0))],
        out_specs=(pl.BlockSpec((t_len, LANES), lambda b: (b, 0)),
                   pl.BlockSpec((None, cap, LANES), lambda b: (b, 0, 0)),
                   pl.BlockSpec((None, nb, LANES), lambda b: (b, 0, 0))),
        scratch_shapes=[pltpu.VMEM((LANES, t_len), F32)],
        compiler_params=_params("parallel"),
        name="topk",
    )(hext, aff.reshape(batch, packed_rows, LANES))
    return slot, idx_t[:, :, :N_EXPERTS].transpose(0, 2, 1), bounds


GATHER_ROWS = 256


def _gather_kernel(idx_ref, h_hbm, xe_ref, g_ref, buf, sem, *, t_len, d, nchunk):
    step = pl.program_id(0)
    rc = xe_ref.shape[0]
    cur = step % 2

    def request(st, slot):
        lst, chunk = st // nchunk, st % nchunk
        row0 = (lst // N_EXPERTS) * t_len

        for r in range(rc):
            row = row0 + idx_ref[lst, chunk * rc + r]
            pltpu.make_async_copy(h_hbm.at[pl.ds(row, 1)], buf.at[slot, pl.ds(r, 1)], sem.at[slot]).start(
                priority=r % 2)

    @pl.when(step == 0)
    def _():
        request(0, 0)

    @pl.when(step + 1 < pl.num_programs(0))
    def _():
        request(step + 1, 1 - cur)

    pltpu.make_async_copy(h_hbm.at[pl.ds(0, rc)], buf.at[cur], sem.at[cur]).wait()
    rows = buf[cur]
    xe_ref[...] = rows[:, :d].astype(xe_ref.dtype)
    aff = rows[:, d:]
    e = (step // nchunk) % N_EXPERTS
    lane = lax.broadcasted_iota(jnp.int32, aff.shape, 1)
    g_ref[...] = jnp.broadcast_to(jnp.sum(jnp.where(lane == e, aff, 0.0), axis=1, keepdims=True), aff.shape)


def _gather(idx, hext, batch, t_len, d, cap):
    rc = min(GATHER_ROWS, cap)
    nchunk = cap // rc

    def out_block(s, idx):
        lst = s // nchunk
        return lst % N_EXPERTS, (lst // N_EXPERTS) * nchunk + s % nchunk, 0

    return pl.pallas_call(
        functools.partial(_gather_kernel, t_len=t_len, d=d, nchunk=nchunk),
        out_shape=(jax.ShapeDtypeStruct((N_EXPERTS, batch * cap, d), BF16),
                   jax.ShapeDtypeStruct((N_EXPERTS, batch * cap, LANES), F32)),
        grid_spec=pltpu.PrefetchScalarGridSpec(
            num_scalar_prefetch=1,
            grid=(batch * N_EXPERTS * nchunk,),
            in_specs=[pl.BlockSpec(memory_space=pl.ANY)],
            out_specs=(pl.BlockSpec((None, rc, d), out_block), pl.BlockSpec((None, rc, LANES), out_block)),
            scratch_shapes=[pltpu.VMEM((2, rc, d + LANES), F32), pltpu.SemaphoreType.DMA((2,))]),
        compiler_params=_params("arbitrary"),
        name="gather",
    )(idx.reshape(batch * N_EXPERTS, cap), hext)


def _expert_up_kernel(x_ref, wg_ref, wu_ref, o_ref):
    x = x_ref[...]
    a = _dot(x, wg_ref[...].astype(BF16))
    u = _dot(x, wu_ref[...].astype(BF16))
    o_ref[...] = (_silu(a) * u).astype(o_ref.dtype)


def _expert_up(xe, w_gate, w_up):
    e, m, d = xe.shape
    ff = w_gate.shape[2]
    tf = 256
    return pl.pallas_call(
        _expert_up_kernel,
        out_shape=jax.ShapeDtypeStruct((e, m, ff), BF16),
        grid=(e, ff // tf),
        in_specs=[pl.BlockSpec((None, m, d), lambda i, f: (i, 0, 0)),
                  pl.BlockSpec((None, d, tf), lambda i, f: (i, 0, f)),
                  pl.BlockSpec((None, d, tf), lambda i, f: (i, 0, f))],
        out_specs=pl.BlockSpec((None, m, tf), lambda i, f: (i, 0, f)),
        compiler_params=_params("parallel", "arbitrary"),
        name="expert_up",
    )(xe, w_gate, w_up)


def _expert_down_kernel(a_ref, w_ref, g_ref, o_ref):
    o_ref[...] = (_dot(a_ref[...], w_ref[...].astype(BF16)) * g_ref[:, :1]).astype(o_ref.dtype)


def _expert_down(act, w_down, g):
    e, m, ff = act.shape
    d = w_down.shape[2]
    tn = min(1024, d)
    return pl.pallas_call(
        _expert_down_kernel,
        out_shape=jax.ShapeDtypeStruct((e, m, d), BF16),
        grid=(e, d // tn),
        in_specs=[pl.BlockSpec((None, m, ff), lambda i, j: (i, 0, 0)),
                  pl.BlockSpec((None, ff, tn), lambda i, j: (i, 0, j)),
                  pl.BlockSpec((None, m, LANES), lambda i, j: (i, 0, 0))],
        out_specs=pl.BlockSpec((None, m, tn), lambda i, j: (i, 0, j)),
        compiler_params=_params("parallel", "arbitrary"),
        name="expert_down",
    )(act, w_down, g)


COMBINE_WINDOW = 64
ROW_ALIGN = 16


def _combine_kernel(bounds_ref, slot_ref, ye_hbm, x_ref, g_ref, fn_ref, o_ref, stage, onehot, sem, *, nt, cap):
    step = pl.program_id(0)
    w = COMBINE_WINDOW
    cur = step % 2

    def tile_rows(st, e):
        b, i = st // nt, st % nt
        return b, bounds_ref[b * (nt + 1) + i, e], bounds_ref[b * (nt + 1) + i + 1, e]

    def window(st, e, r):
        b, first, _ = tile_rows(st, e)
        base = (first // ROW_ALIGN) * ROW_ALIGN + r * w
        return b, base, jnp.minimum(base, cap - w)

    def window_copy(st, e, r, buf):
        b, _, src = window(st, e, r)
        return pltpu.make_async_copy(ye_hbm.at[e, pl.ds(pl.multiple_of(b * cap + src, ROW_ALIGN), w), :],
                                     stage.at[buf, pl.ds(e * w, w), :], sem.at[buf])

    def start_round(st, r, buf):
        for e in range(N_EXPERTS):
            window_copy(st, e, r, buf).start()

    def wait_round(st, r, buf):
        for e in range(N_EXPERTS):
            window_copy(st, e, r, buf).wait()

    def scatter(r):
        slots = slot_ref[...]
        pos = lax.broadcasted_iota(jnp.int32, (1, w), 1)
        for e in range(N_EXPERTS):
            _, base, src = window(step, e, r)
            col = slots[:, e:e + 1]
            hit = (col >= base) & (col - src == pos)
            onehot[:, e * w:(e + 1) * w] = jnp.where(hit, 1.0, 0.0).astype(BF16)
        return _dot(onehot[...], stage[cur])

    @pl.when(step == 0)
    def _():
        start_round(0, 0, 0)

    @pl.when(step + 1 < pl.num_programs(0))
    def _():
        start_round(step + 1, 0, 1 - cur)

    wait_round(step, 0, cur)
    o_ref[...] = scatter(0)

    rounds = 1
    for e in range(N_EXPERTS):
        _, first, last = tile_rows(step, e)
        rounds = jnp.maximum(rounds, (last - (first // ROW_ALIGN) * ROW_ALIGN + w - 1) // w)

    def extra_round(r, carry):
        start_round(step, r, cur)
        wait_round(step, r, cur)
        o_ref[...] += scatter(r)
        return carry

    lax.fori_loop(1, rounds, extra_round, 0)

    v = x_ref[...] + g_ref[...] * o_ref[...]
    y = v * lax.rsqrt(jnp.mean(v * v, axis=-1, keepdims=True) + NORM_EPS)
    o_ref[...] = y * fn_ref[...]


def _combine(bounds, slot, ye, x2d, mod4, final_norm, batch, t_len, cap, k_gate):
    m, d = x2d.shape
    tm = TOPK_TILE
    nt = t_len // tm
    w = COMBINE_WINDOW
    return pl.pallas_call(
        functools.partial(_combine_kernel, nt=nt, cap=cap),
        out_shape=jax.ShapeDtypeStruct((m, d), F32),
        grid_spec=pltpu.PrefetchScalarGridSpec(
            num_scalar_prefetch=1,
            grid=(batch * nt,),
            in_specs=[pl.BlockSpec((tm, LANES), lambda s, bnd: (s, 0)),
                      pl.BlockSpec(memory_space=pl.ANY),
                      pl.BlockSpec((tm, d), lambda s, bnd: (s, 0)),
                      pl.BlockSpec((None, None, 1, d), lambda s, bnd: (s // nt, k_gate, 0, 0)),
                      pl.BlockSpec((1, d), lambda s, bnd: (0, 0))],
            out_specs=pl.BlockSpec((tm, d), lambda s, bnd: (s, 0)),
            scratch_shapes=[pltpu.VMEM((2, N_EXPERTS * w, d), BF16), pltpu.VMEM((tm, N_EXPERTS * w), BF16),
                            pltpu.SemaphoreType.DMA((2,))]),
        compiler_params=_params("arbitrary"),
        name="combine",
    )(bounds.reshape(batch * (nt + 1), LANES), slot, ye, x2d, mod4, final_norm.reshape(1, d))


def kernel(x, c, ctx, c_ctx, norm1, norm2, w_mod, b_mod, w_in, na_rpb, ret_decay, w_branch_na, w_branch_ret,
           w_out, w_router, w_gate, w_up, w_down, final_norm):
    batch, t_len, d = x.shape
    tc_len = ctx.shape[1]
    na_w = NA_HEADS * NA_HEAD_DIM
    ret_w = RET_HEADS * RET_DIM
    col_qa, col_ka, col_va = 0, na_w, 2 * na_w
    col_qr = 3 * na_w
    col_kr, col_vr, col_gf, col_gb = col_qr + ret_w, col_qr + 2 * ret_w, col_qr + 3 * ret_w, col_qr + 4 * ret_w
    col_ga = col_qr + 5 * ret_w
    col_gb2 = col_ga + d
    cap = EC_CAPACITY_FACTOR * t_len // N_EXPERTS
    assert w_in.shape[0] == 1, "single layer"

    x2d = x.reshape(batch * t_len, d)
    cvec = jnp.concatenate([c, c_ctx[None], jnp.zeros((8 - batch - 1, d), F32)], axis=0)
    mod = _modulation(cvec, w_mod[0], b_mod[0])
    mod4 = mod[:batch + 1].reshape(batch + 1, N_MOD, 1, d)

    h = _prenorm(x2d, norm1[0], mod4, t_len, 0, 0)
    hc = _prenorm(ctx.reshape(batch * tc_len, d), norm1[0], mod4, batch * tc_len, batch, 0)
    tn = 512
    proj = _matmul(h, w_in[0], w_in.shape[2], 2048, tn, lambda j: j, "in_proj")
    kv_w = 2 * na_w
    cproj = _matmul(hc, w_in[0], kv_w + 2 * ret_w, batch * tc_len, tn,
                    lambda j: jnp.where(j < kv_w // tn, col_ka // tn + j, col_kr // tn + j - kv_w // tn), "ctx_proj")

    y_na = _na_attention(proj, cproj, na_rpb[0], batch, t_len, tc_len, col_qa, col_ka, col_va)
    y_ret = _retention(proj, cproj, ret_decay[0], batch, t_len, tc_len, col_qr, col_kr, col_vr, col_gf, col_gb,
                       kv_w, kv_w + ret_w)
    mixed = _merge(y_na, y_ret, w_branch_na[0], w_branch_ret[0], proj, col_ga, col_gb2)
    x1 = _outproj(mixed, w_out[0], x2d, mod4, t_len, 2)

    w_router_pad = jnp.pad(w_router[0], ((0, 0), (0, LANES - N_EXPERTS))).astype(BF16)
    hext, aff = _router(x1, norm2[0], mod4, w_router_pad, t_len, 3)
    slot, idx, bounds = _topk(hext, aff, batch, t_len, d, cap)
    xe, g = _gather(idx, hext, batch, t_len, d, cap)
    act = _expert_up(xe, w_gate[0], w_up[0])
    ye = _expert_down(act, w_down[0], g)
    out = _combine(bounds, slot, ye, x1, mod4, final_norm, batch, t_len, cap, 5)
    return out.reshape(batch, t_len, d)
```

```python
import functools

import numpy as np
import jax
import jax.numpy as jnp
from jax import lax
from jax.experimental import pallas as pl
from jax.experimental.pallas import tpu as pltpu

F32 = jnp.float32
BF16 = jnp.bfloat16

GRID_W = 64
NA_HEADS = 16
NA_HEAD_DIM = 128
NA_WIN_R = 8
NA_WIN_C = 16
RET_HEADS = 8
RET_DIM = 256
RET_CHUNK = 256
N_EXPERTS = 16
EC_CAPACITY_FACTOR = 2
ROPE_BASE = 10000.0
NORM_EPS = 1e-6
NEG_INF = -1e30
N_MOD = 6

VMEM_LIMIT_BYTES = 56 * 1024 * 1024
LANES = 128

NA_Q_ROWS = 4
NA_K_ROWS = NA_Q_ROWS + NA_WIN_R


def _params(*sem):
    return pltpu.CompilerParams(dimension_semantics=sem, vmem_limit_bytes=VMEM_LIMIT_BYTES)


def _dot(a, b):
    return jnp.dot(a, b, preferred_element_type=F32)


def _dot_nt(a, b):
    return lax.dot_general(a, b, (((1,), (1,)), ((), ())), preferred_element_type=F32)


def _dot_tn(a, b):
    return lax.dot_general(a, b, (((0,), (0,)), ((), ())), preferred_element_type=F32)


def _silu(x):
    return x * jax.nn.sigmoid(x)


def _mod_kernel(c_ref, w_ref, b_ref, o_ref):
    a = _silu(c_ref[...]).astype(BF16)
    o_ref[...] = _dot(a, w_ref[...].astype(BF16)) + b_ref[...]


def _modulation(cvec, w_mod, b_mod):
    r, d = cvec.shape
    n = w_mod.shape[1]
    tn = 1024
    return pl.pallas_call(
        _mod_kernel,
        out_shape=jax.ShapeDtypeStruct((r, n), F32),
        grid=(n // tn,),
        in_specs=[pl.BlockSpec((r, d), lambda j: (0, 0)),
                  pl.BlockSpec((d, tn), lambda j: (0, j)),
                  pl.BlockSpec((1, tn), lambda j: (0, j))],
        out_specs=pl.BlockSpec((r, tn), lambda j: (0, j)),
        compiler_params=_params("arbitrary"),
        name="modulation",
    )(cvec, w_mod, b_mod.reshape(1, n))


def _modulated_norm(x, g, shift, scale):
    y = x * lax.rsqrt(jnp.mean(x * x, axis=-1, keepdims=True) + NORM_EPS)
    return (y * g) * (1.0 + scale) + shift


def _prenorm_kernel(x_ref, g_ref, sh_ref, sc_ref, o_ref):
    o_ref[...] = _modulated_norm(x_ref[...], g_ref[...], sh_ref[...], sc_ref[...]).astype(o_ref.dtype)


def _prenorm(x2d, gain, mod4, rows_per_sample, sample0, k_shift):
    r, d = x2d.shape
    tr = 512
    per = rows_per_sample // tr
    return pl.pallas_call(
        _prenorm_kernel,
        out_shape=jax.ShapeDtypeStruct((r, d), BF16),
        grid=(r // tr,),
        in_specs=[pl.BlockSpec((tr, d), lambda i: (i, 0)),
                  pl.BlockSpec((1, d), lambda i: (0, 0)),
                  pl.BlockSpec((None, None, 1, d), lambda i: (sample0 + i // per, k_shift, 0, 0)),
                  pl.BlockSpec((None, None, 1, d), lambda i: (sample0 + i // per, k_shift + 1, 0, 0))],
        out_specs=pl.BlockSpec((tr, d), lambda i: (i, 0)),
        compiler_params=_params("parallel"),
        name="prenorm",
    )(x2d, gain.reshape(1, d), mod4, mod4)


def _mm_kernel(a_ref, w_ref, o_ref):
    o_ref[...] = _dot(a_ref[...], w_ref[...].astype(BF16)).astype(o_ref.dtype)


def _matmul(a, w, n_out, tm, tn, col_block, name):
    m, k = a.shape
    return pl.pallas_call(
        _mm_kernel,
        out_shape=jax.ShapeDtypeStruct((m, n_out), BF16),
        grid=(m // tm, n_out // tn),
        in_specs=[pl.BlockSpec((tm, k), lambda i, j: (i, 0), pipeline_mode=pl.Buffered(1)),
                  pl.BlockSpec((k, tn), lambda i, j: (0, col_block(j)))],
        out_specs=pl.BlockSpec((tm, tn), lambda i, j: (i, j)),
        compiler_params=_params("parallel", "arbitrary"),
        name=name,
    )(a, w)


def _na_tables(rows):
    wr = NA_WIN_R
    bases, tables = [], []
    for t in range(rows // NA_Q_ROWS):
        kb = int(np.clip(NA_Q_ROWS * t - wr // 2, 0, rows - NA_K_ROWS))
        tab = []
        for i in range(NA_Q_ROWS):
            r = NA_Q_ROWS * t + i
            r0 = int(np.clip(r - wr // 2, 0, rows - wr))
            tab.append(tuple((kb + j - r + NA_WIN_R - 1) if r0 <= kb + j < r0 + wr else None
                             for j in range(NA_K_ROWS)))
        bases.append(kb)
        tables.append(tuple(tab))
    uniq = list(dict.fromkeys(tables))
    return bases, uniq, [uniq.index(t) for t in tables]


def _na_bias_diagonals(rpb):
    offset = np.clip(np.arange(LANES) - (GRID_W - 1), -(NA_WIN_C - 1), NA_WIN_C - 1) + (NA_WIN_C - 1)
    return rpb.astype(F32)[:, :, offset]


def _na_build_bias(diag_ref, rowbias_ref, bias_ref, patterns):
    assert LANES == 2 * GRID_W
    shape = (GRID_W, LANES)
    qc = lax.broadcasted_iota(jnp.int32, shape, 0)
    lane = lax.broadcasted_iota(jnp.int32, shape, 1)
    kc = lane % GRID_W
    c0 = jnp.clip(qc - NA_WIN_C // 2, 0, GRID_W - NA_WIN_C)
    col_ok = (kc >= c0) & (kc < c0 + NA_WIN_C)
    low = lane < GRID_W
    for dr in range(diag_ref.shape[0]):
        diag = jnp.broadcast_to(diag_ref[dr:dr + 1, :], shape)
        first = pltpu.roll(diag, GRID_W + 1, 1, stride=1, stride_axis=0)
        second = pltpu.roll(diag, 1, 1, stride=1, stride_axis=0)
        rowbias_ref[dr] = jnp.where(col_ok, jnp.where(low, first, second), NEG_INF)
    outside = jnp.full(shape, NEG_INF, F32)
    block = lambda dr: outside if dr is None else rowbias_ref[dr]
    for p, tab in enumerate(patterns):
        for i, row in enumerate(tab):
            for j in range(0, NA_K_ROWS, 2):
                bias_ref[p, i * GRID_W:(i + 1) * GRID_W, j * GRID_W:(j + 2) * GRID_W] = jnp.where(
                    low, block(row[j]), block(row[j + 1]))


def _na_kernel(q_ref, k_ref, v_ref, kc_ref, vc_ref, diag_ref, o_ref, rowbias_ref, bias_ref, vext_ref, vcext_ref, *,
               rows, patterns, pat_ids):
    tq = NA_Q_ROWS * GRID_W
    nk = NA_K_ROWS * GRID_W
    scale = NA_HEAD_DIM ** -0.5
    for src, dst in ((v_ref, vext_ref), (vc_ref, vcext_ref)):
        dst[:, :NA_HEAD_DIM] = src[...]
        dst[:, NA_HEAD_DIM:] = jnp.ones_like(src)

    @pl.when(pl.program_id(1) == 0)
    def _():
        _na_build_bias(diag_ref, rowbias_ref, bias_ref, patterns)

    def tile(t, carry):
        kb = pl.multiple_of(jnp.clip(NA_Q_ROWS * t - NA_WIN_R // 2, 0, rows - NA_K_ROWS) * GRID_W, tq)
        q0 = pl.multiple_of(t * tq, tq)
        pat = 0
        for i, pid in enumerate(pat_ids):
            pat = jnp.where(t == i, pid, pat)
        q = q_ref[pl.ds(q0, tq), :]
        s_w = _dot_nt(q, k_ref[pl.ds(kb, nk), :]) * scale + bias_ref[pat]
        s_c = _dot_nt(q, kc_ref[...]) * scale
        m = jnp.maximum(jnp.max(s_w, axis=-1, keepdims=True), jnp.max(s_c, axis=-1, keepdims=True))
        p_w = jnp.exp(s_w - m)
        p_c = jnp.exp(s_c - m)
        o = _dot(p_w.astype(BF16), vext_ref[pl.ds(kb, nk), :]) + _dot(p_c.astype(BF16), vcext_ref[...])
        dh = NA_HEAD_DIM
        o_ref[pl.ds(q0, tq), :] = (o[:, :dh] / o[:, dh:dh + 1]).astype(o_ref.dtype)
        return carry

    lax.fori_loop(0, rows // NA_Q_ROWS, tile, 0, unroll=16)


def _na_attention(proj, cproj, rpb, batch, t_len, tc_len, col_q, col_k, col_v):
    rows = t_len // GRID_W
    tq = NA_Q_ROWS * GRID_W
    _, patterns, pat_ids = _na_tables(rows)
    diag = _na_bias_diagonals(rpb)
    n_dr = diag.shape[1]
    dh = NA_HEAD_DIM
    tok = lambda col: pl.BlockSpec((t_len, dh), lambda h, b: (b, col // dh + h))
    return pl.pallas_call(
        functools.partial(_na_kernel, rows=rows, patterns=tuple(patterns), pat_ids=tuple(pat_ids)),
        out_shape=jax.ShapeDtypeStruct((batch * t_len, NA_HEADS * dh), BF16),
        grid=(NA_HEADS, batch),
        in_specs=[tok(col_q), tok(col_k), tok(col_v),
                  pl.BlockSpec((tc_len, dh), lambda h, b: (b, h)),
                  pl.BlockSpec((tc_len, dh), lambda h, b: (b, NA_HEADS + h)),
                  pl.BlockSpec((None, n_dr, LANES), lambda h, b: (h, 0, 0))],
        out_specs=pl.BlockSpec((t_len, dh), lambda h, b: (b, h)),
        scratch_shapes=[pltpu.VMEM((n_dr, GRID_W, LANES), F32),
                        pltpu.VMEM((len(patterns), tq, NA_K_ROWS * GRID_W), F32),
                        pltpu.VMEM((t_len, 2 * dh), BF16), pltpu.VMEM((tc_len, 2 * dh), BF16)],
        compiler_params=_params("arbitrary", "arbitrary"),
        name="na_attention",
    )(proj, proj, proj, cproj, cproj, diag)


def _rope_tables(t_len):
    quarter = RET_DIM // 4
    inv = ROPE_BASE ** (-np.arange(quarter, dtype=np.float64) / quarter)
    tpos = np.arange(t_len)
    row_ang = (tpos // GRID_W).astype(np.float64)[:, None] * inv[None, :]
    col_ang = (tpos % GRID_W).astype(np.float64)[:, None] * inv[None, :]
    cos = np.concatenate([np.cos(row_ang)] * 2 + [np.cos(col_ang)] * 2, axis=-1)
    sin = np.concatenate([-np.sin(row_ang), np.sin(row_ang), -np.sin(col_ang), np.sin(col_ang)], axis=-1)
    return jnp.asarray(cos, F32), jnp.asarray(sin, F32)


def _rope(a, cos, sin):
    half = RET_DIM // 2
    swapped = jnp.concatenate([pltpu.roll(a[:, :half], half // 2, 1), pltpu.roll(a[:, half:], half // 2, 1)], axis=1)
    return a * cos + swapped * sin


def _ret_kernel(dec_ref, q_ref, k_ref, v_ref, gf_ref, gb_ref, kc_ref, vc_ref, cos_ref, sin_ref, o_ref,
                sf_ref, sb_ref, acc_ref, qrot_ref, krot_ref):
    h = pl.program_id(1)
    c = RET_CHUNK
    t_len = q_ref.shape[0]
    tc_len = kc_ref.shape[0]
    nc = t_len // c
    k_scale = RET_DIM ** -0.5

    def log_gamma(direction):
        e = jnp.full((1, 1), dec_ref[direction, h], F32)
        return jnp.log1p(-jnp.exp2(-e))

    lg_f, lg_b = log_gamma(0), log_gamma(1)
    pos = lax.broadcasted_iota(jnp.int32, (c, 1), 0).astype(F32)
    diff = pos - lax.broadcasted_iota(jnp.int32, (1, c), 1).astype(F32)
    dec_f = jnp.where(diff >= 0, jnp.exp(lg_f * jnp.maximum(diff, 0.0)), 0.0) * k_scale
    dec_b = jnp.where(diff <= 0, jnp.exp(lg_b * jnp.maximum(-diff, 0.0)), 0.0) * k_scale
    qdec_f, kdec_f, cdec_f = jnp.exp(lg_f * (pos + 1.0)), jnp.exp(lg_f * (c - 1.0 - pos)) * k_scale, jnp.exp(lg_f * c)
    qdec_b, kdec_b, cdec_b = jnp.exp(lg_b * (c - pos)), jnp.exp(lg_b * pos) * k_scale, jnp.exp(lg_b * c)

    cpos = lax.broadcasted_iota(jnp.int32, (tc_len, 1), 0).astype(F32)
    kc = kc_ref[...].astype(F32) * k_scale
    vc = vc_ref[...]
    sf_ref[...] = _dot_tn((kc * jnp.exp(lg_f * (tc_len - 1.0 - cpos))).astype(BF16), vc)
    sb_ref[...] = _dot_tn((kc * jnp.exp(lg_b * cpos)).astype(BF16), vc)

    def chunk(n, s_ref, dec, qdec, kdec, cdec, g_ref, first_visit):
        rows = pl.ds(pl.multiple_of(n * c, c), c)
        if first_visit:
            cos, sin = cos_ref[rows, :], sin_ref[rows, :]
            qb = _rope(q_ref[rows, :].astype(F32), cos, sin).astype(BF16)
            k = _rope(k_ref[rows, :].astype(F32), cos, sin)
            qrot_ref[rows, :] = qb
            krot_ref[rows, :] = k
        else:
            qb, k = qrot_ref[rows, :], krot_ref[rows, :]
        v = v_ref[rows, :]
        scores = _dot_nt(qb, k.astype(BF16)) * dec
        s = s_ref[...]
        o = _dot(scores.astype(BF16), v) + _dot(qb, s.astype(BF16)) * qdec
        s_ref[...] = s * cdec + _dot_tn((k * kdec).astype(BF16), v)
        on = o * lax.rsqrt(jnp.mean(o * o, axis=-1, keepdims=True) + NORM_EPS)
        gated = _silu(g_ref[rows, :].astype(F32)) * on
        if first_visit:
            acc_ref[rows, :] = gated
        else:
            acc_ref[rows, :] += gated

    assert nc % 2 == 0

    def steps(first_visit):
        def body(n, carry):
            chunk(n, sf_ref, dec_f, qdec_f, kdec_f, cdec_f, gf_ref, first_visit)
            chunk(nc - 1 - n, sb_ref, dec_b, qdec_b, kdec_b, cdec_b, gb_ref, first_visit)
            return carry
        return body

    lax.fori_loop(0, nc // 2, steps(True), 0, unroll=2)
    lax.fori_loop(nc // 2, nc, steps(False), 0, unroll=8)
    o_ref[...] = acc_ref[...].astype(o_ref.dtype)


def _retention(proj, cproj, ret_decay, batch, t_len, tc_len, col_q, col_k, col_v, col_gf, col_gb, ccol_k, ccol_v):
    d = RET_DIM
    cos, sin = _rope_tables(t_len)
    tok = lambda col: pl.BlockSpec((t_len, d), lambda b, h: (b, col // d + h))
    ctx = lambda col: pl.BlockSpec((tc_len, d), lambda b, h: (b, col // d + h))
    tab = pl.BlockSpec((t_len, d), lambda b, h: (0, 0))
    return pl.pallas_call(
        _ret_kernel,
        out_shape=jax.ShapeDtypeStruct((batch * t_len, RET_HEADS * d), BF16),
        grid=(batch, RET_HEADS),
        in_specs=[pl.BlockSpec(memory_space=pltpu.SMEM),
                  tok(col_q), tok(col_k), tok(col_v), tok(col_gf), tok(col_gb), ctx(ccol_k), ctx(ccol_v), tab, tab],
        out_specs=pl.BlockSpec((t_len, d), lambda b, h: (b, h)),
        scratch_shapes=[pltpu.VMEM((d, d), F32), pltpu.VMEM((d, d), F32), pltpu.VMEM((t_len, d), F32),
                        pltpu.VMEM((t_len, d), BF16), pltpu.VMEM((t_len, d), F32)],
        compiler_params=_params("parallel", "arbitrary"),
        name="retention",
    )(ret_decay.astype(F32), proj, proj, proj, proj, proj, cproj, cproj, cos, sin)


def _merge_kernel(ya_ref, yr_ref, wa_ref, wr_ref, ga_ref, gb_ref, o_ref):
    a = _dot(ya_ref[...], wa_ref[...].astype(BF16))
    r = _dot(yr_ref[...], wr_ref[...].astype(BF16))
    o_ref[...] = (jax.nn.sigmoid(ga_ref[...].astype(F32)) * a + jax.nn.sigmoid(gb_ref[...].astype(F32)) * r
                  ).astype(o_ref.dtype)


def _merge(y_na, y_ret, w_na, w_ret, proj, col_ga, col_gb):
    m, ka = y_na.shape
    kr = y_ret.shape[1]
    n = w_na.shape[1]
    tm, tn = 1024, 512
    return pl.pallas_call(
        _merge_kernel,
        out_shape=jax.ShapeDtypeStruct((m, n), BF16),
        grid=(m // tm, n // tn),
        in_specs=[pl.BlockSpec((tm, ka), lambda i, j: (i, 0)),
                  pl.BlockSpec((tm, kr), lambda i, j: (i, 0)),
                  pl.BlockSpec((ka, tn), lambda i, j: (0, j)),
                  pl.BlockSpec((kr, tn), lambda i, j: (0, j)),
                  pl.BlockSpec((tm, tn), lambda i, j: (i, col_ga // tn + j)),
                  pl.BlockSpec((tm, tn), lambda i, j: (i, col_gb // tn + j))],
        out_specs=pl.BlockSpec((tm, tn), lambda i, j: (i, j)),
        compiler_params=_params("parallel", "arbitrary"),
        name="merge",
    )(y_na, y_ret, w_na, w_ret, proj, proj)


def _outproj_kernel(m_ref, w_ref, x_ref, g_ref, o_ref):
    o_ref[...] = x_ref[...] + g_ref[...] * _dot(m_ref[...], w_ref[...].astype(BF16))


def _outproj(mixed, w_out, x2d, mod4, rows_per_sample, k_gate):
    m, k = mixed.shape
    n = w_out.shape[1]
    tm, tn = 1024, 512
    per = rows_per_sample // tm
    return pl.pallas_call(
        _outproj_kernel,
        out_shape=jax.ShapeDtypeStruct((m, n), F32),
        grid=(m // tm, n // tn),
        in_specs=[pl.BlockSpec((tm, k), lambda i, j: (i, 0)),
                  pl.BlockSpec((k, tn), lambda i, j: (0, j)),
                  pl.BlockSpec((tm, tn), lambda i, j: (i, j)),
                  pl.BlockSpec((None, None, 1, tn), lambda i, j: (i // per, k_gate, 0, j))],
        out_specs=pl.BlockSpec((tm, tn), lambda i, j: (i, j)),
        compiler_params=_params("parallel", "arbitrary"),
        name="outproj",
    )(mixed, w_out, x2d, mod4)


def _router_kernel(x_ref, g_ref, sh_ref, sc_ref, wr_ref, o_ref, a_ref):
    d = x_ref.shape[1]
    h = _modulated_norm(x_ref[...], g_ref[...], sh_ref[...], sc_ref[...])
    logits = _dot(h.astype(BF16), wr_ref[...])
    lane = lax.broadcasted_iota(jnp.int32, logits.shape, 1)
    logits = jnp.where(lane < N_EXPERTS, logits, NEG_INF)
    p = jnp.exp(logits - jnp.max(logits, axis=-1, keepdims=True))
    aff = p / jnp.sum(p, axis=-1, keepdims=True)
    o_ref[:, :d] = h
    o_ref[:, d:] = aff
    a_ref[...] = aff[:, :N_EXPERTS]


def _router(x2d, gain, mod4, w_router_pad, rows_per_sample, k_shift):
    r, d = x2d.shape
    tr = 512
    per = rows_per_sample // tr
    return pl.pallas_call(
        _router_kernel,
        out_shape=(jax.ShapeDtypeStruct((r, d + LANES), F32), jax.ShapeDtypeStruct((r, N_EXPERTS), F32)),
        grid=(r // tr,),
        in_specs=[pl.BlockSpec((tr, d), lambda i: (i, 0)),
                  pl.BlockSpec((1, d), lambda i: (0, 0)),
                  pl.BlockSpec((None, None, 1, d), lambda i: (i // per, k_shift, 0, 0)),
                  pl.BlockSpec((None, None, 1, d), lambda i: (i // per, k_shift + 1, 0, 0)),
                  pl.BlockSpec((d, LANES), lambda i: (0, 0))],
        out_specs=(pl.BlockSpec((tr, d + LANES), lambda i: (i, 0)), pl.BlockSpec((tr, N_EXPERTS), lambda i: (i, 0))),
        compiler_params=_params("parallel"),
        name="router",
    )(x2d, gain.reshape(1, d), mod4, mod4, w_router_pad)


TOPK_TILE = 256
GEOMETRIC_STEPS = 32
ARITHMETIC_STEPS = 12


def _topk_kernel(aff_ref, affc_ref, slot_ref, idx_ref, bounds_ref, slot_t_ref, *, cap):
    t_len = aff_ref.shape[0]
    tt = TOPK_TILE
    nt = t_len // tt
    packed = affc_ref[...]

    def per_expert(v):
        shift = LANES // 2
        while shift >= N_EXPERTS:
            v = v + pltpu.roll(v, shift, 1)
            shift //= 2
        return v

    def narrow(c, mid):
        lo, hi = c
        ge = per_expert(jnp.sum(jnp.where(packed >= mid, 1.0, 0.0), axis=0, keepdims=True)) >= cap
        return jnp.where(ge, mid, lo), jnp.where(ge, hi, mid)

    tiny = float(np.finfo(np.float32).tiny)
    above_tiny = per_expert(jnp.sum(jnp.where(packed >= tiny, 1.0, 0.0), axis=0, keepdims=True)) >= cap
    bracket = (jnp.where(above_tiny, tiny, 0.0), jnp.where(above_tiny, 2.0, tiny) + jnp.zeros((1, LANES), F32))
    geometric_mid = lambda c: jnp.clip(jnp.sqrt(c[0]) * jnp.sqrt(c[1]), c[0], c[1])
    bracket = lax.fori_loop(0, GEOMETRIC_STEPS, lambda _, c: narrow(c, geometric_mid(c)), bracket)
    lo, hi = lax.fori_loop(0, ARITHMETIC_STEPS, lambda _, c: narrow(c, 0.5 * (c[0] + c[1])), bracket)

    def count_above(i, cnt):
        r0 = pl.multiple_of(i * tt, tt)
        return cnt + jnp.sum(jnp.where(aff_ref[pl.ds(r0, tt), :] >= hi, 1.0, 0.0), axis=0, keepdims=True)
    need = cap - lax.fori_loop(0, nt, count_above, jnp.zeros((1, LANES), F32))

    tri = jnp.where(lax.broadcasted_iota(jnp.int32, (tt, tt), 0) >= lax.broadcasted_iota(jnp.int32, (tt, tt), 1),
                    1.0, 0.0).astype(BF16)

    def assign(i, carry):
        eq_before, sel_before = carry
        r0 = pl.multiple_of(i * tt, tt)
        a = aff_ref[pl.ds(r0, tt), :]
        above = a >= hi
        tie = (a >= lo) & (a < hi)
        eq = jnp.where(tie, 1.0, 0.0)
        eq_rank = _dot(tri, eq.astype(BF16)) + eq_before
        sel = jnp.where(above | (tie & (eq_rank <= need)), 1.0, 0.0)
        sel_rank = _dot(tri, sel.astype(BF16)) + sel_before
        slot_ref[pl.ds(r0, tt), :] = jnp.where(sel > 0, sel_rank - 1.0, -1.0).astype(jnp.int32)
        bounds_ref[pl.ds(i, 1), :] = sel_before.astype(jnp.int32)
        return (eq_before + jnp.sum(eq, axis=0, keepdims=True), sel_before + jnp.sum(sel, axis=0, keepdims=True))

    zero = jnp.zeros((1, LANES), F32)
    _, total = lax.fori_loop(0, nt, assign, (zero, zero), unroll=2)
    bounds_ref[nt:nt + 1, :] = total.astype(jnp.int32)

    slot_t_ref[...] = jnp.transpose(slot_ref[...].astype(F32))
    idx_ref[...] = jnp.zeros_like(idx_ref)
    sublanes = 8
    tok = lax.broadcasted_iota(jnp.int32, (sublanes, t_len), 1).astype(F32)
    sub = lax.broadcasted_iota(jnp.int32, (sublanes, 1), 0).astype(F32)
    for e in range(N_EXPERTS):
        def body(g, carry):
            s0 = pl.multiple_of(g * sublanes, sublanes)
            hit = slot_t_ref[e:e + 1, :] == sub + s0.astype(F32)
            idx_ref[pl.ds(s0, sublanes), e:e + 1] = jnp.sum(jnp.where(hit, tok, 0.0), axis=1,
                                                            keepdims=True).astype(jnp.int32)
            return carry
        lax.fori_loop(0, cap // sublanes, body, 0, unroll=8)


def _topk(hext, aff, batch, t_len, d, cap):
    nb = t_len // TOPK_TILE + 1
    packed_rows = t_len * N_EXPERTS // LANES
    slot, idx_t, bounds = pl.pallas_call(
        functools.partial(_topk_kernel, cap=cap),
        out_shape=(jax.ShapeDtypeStruct((batch * t_len, LANES), jnp.int32),
                   jax.ShapeDtypeStruct((batch, cap, LANES), jnp.int32),
                   jax.ShapeDtypeStruct((batch, nb, LANES), jnp.int32)),
        grid=(batch,),
        in_specs=[pl.BlockSpec((t_len, LANES), lambda b: (b, d // LANES)),
                  pl.BlockSpec((None, packed_rows, LANES), lambda b: (b, 0, 0))],
        out_specs=(pl.BlockSpec((t_len, LANES), lambda b: (b, 0)),
                   pl.BlockSpec((None, cap, LANES), lambda b: (b, 0, 0)),
                   pl.BlockSpec((None, nb, LANES), lambda b: (b, 0, 0))),
        scratch_shapes=[pltpu.VMEM((LANES, t_len), F32)],
        compiler_params=_params("parallel"),
        name="topk",
    )(hext, aff.reshape(batch, packed_rows, LANES))
    return slot, idx_t[:, :, :N_EXPERTS].transpose(0, 2, 1), bounds


GATHER_ROWS = 256


def _gather_kernel(idx_ref, h_hbm, xe_ref, g_ref, buf, sem, *, t_len, d, nchunk):
    step = pl.program_id(0)
    rc = xe_ref.shape[0]
    cur = step % 2

    def request(st, slot):
        lst, chunk = st // nchunk, st % nchunk
        row0 = (lst // N_EXPERTS) * t_len

        for r in range(rc):
            row = row0 + idx_ref[lst, chunk * rc + r]
            pltpu.make_async_copy(h_hbm.at[pl.ds(row, 1)], buf.at[slot, pl.ds(r, 1)], sem.at[slot]).start(
                priority=r % 2)

    @pl.when(step == 0)
    def _():
        request(0, 0)

    @pl.when(step + 1 < pl.num_programs(0))
    def _():
        request(step + 1, 1 - cur)

    pltpu.make_async_copy(h_hbm.at[pl.ds(0, rc)], buf.at[cur], sem.at[cur]).wait()
    rows = buf[cur]
    xe_ref[...] = rows[:, :d].astype(xe_ref.dtype)
    aff = rows[:, d:]
    e = (step // nchunk) % N_EXPERTS
    lane = lax.broadcasted_iota(jnp.int32, aff.shape, 1)
    g_ref[...] = jnp.broadcast_to(jnp.sum(jnp.where(lane == e, aff, 0.0), axis=1, keepdims=True), aff.shape)


def _gather(idx, hext, batch, t_len, d, cap):
    rc = min(GATHER_ROWS, cap)
    nchunk = cap // rc

    def out_block(s, idx):
        lst = s // nchunk
        return lst % N_EXPERTS, (lst // N_EXPERTS) * nchunk + s % nchunk, 0

    return pl.pallas_call(
        functools.partial(_gather_kernel, t_len=t_len, d=d, nchunk=nchunk),
        out_shape=(jax.ShapeDtypeStruct((N_EXPERTS, batch * cap, d), BF16),
                   jax.ShapeDtypeStruct((N_EXPERTS, batch * cap, LANES), F32)),
        grid_spec=pltpu.PrefetchScalarGridSpec(
            num_scalar_prefetch=1,
            grid=(batch * N_EXPERTS * nchunk,),
            in_specs=[pl.BlockSpec(memory_space=pl.ANY)],
            out_specs=(pl.BlockSpec((None, rc, d), out_block), pl.BlockSpec((None, rc, LANES), out_block)),
            scratch_shapes=[pltpu.VMEM((2, rc, d + LANES), F32), pltpu.SemaphoreType.DMA((2,))]),
        compiler_params=_params("arbitrary"),
        name="gather",
    )(idx.reshape(batch * N_EXPERTS, cap), hext)


def _expert_up_kernel(x_ref, wg_ref, wu_ref, o_ref):
    x = x_ref[...]
    a = _dot(x, wg_ref[...].astype(BF16))
    u = _dot(x, wu_ref[...].astype(BF16))
    o_ref[...] = (_silu(a) * u).astype(o_ref.dtype)


def _expert_up(xe, w_gate, w_up):
    e, m, d = xe.shape
    ff = w_gate.shape[2]
    tf = 256
    return pl.pallas_call(
        _expert_up_kernel,
        out_shape=jax.ShapeDtypeStruct((e, m, ff), BF16),
        grid=(e, ff // tf),
        in_specs=[pl.BlockSpec((None, m, d), lambda i, f: (i, 0, 0)),
                  pl.BlockSpec((None, d, tf), lambda i, f: (i, 0, f)),
                  pl.BlockSpec((None, d, tf), lambda i, f: (i, 0, f))],
        out_specs=pl.BlockSpec((None, m, tf), lambda i, f: (i, 0, f)),
        compiler_params=_params("parallel", "arbitrary"),
        name="expert_up",
    )(xe, w_gate, w_up)


def _expert_down_kernel(a_ref, w_ref, g_ref, o_ref):
    o_ref[...] = (_dot(a_ref[...], w_ref[...].astype(BF16)) * g_ref[:, :1]).astype(o_ref.dtype)


def _expert_down(act, w_down, g):
    e, m, ff = act.shape
    d = w_down.shape[2]
    tn = min(1024, d)
    return pl.pallas_call(
        _expert_down_kernel,
        out_shape=jax.ShapeDtypeStruct((e, m, d), BF16),
        grid=(e, d // tn),
        in_specs=[pl.BlockSpec((None, m, ff), lambda i, j: (i, 0, 0)),
                  pl.BlockSpec((None, ff, tn), lambda i, j: (i, 0, j)),
                  pl.BlockSpec((None, m, LANES), lambda i, j: (i, 0, 0))],
        out_specs=pl.BlockSpec((None, m, tn), lambda i, j: (i, 0, j)),
        compiler_params=_params("parallel", "arbitrary"),
        name="expert_down",
    )(act, w_down, g)


COMBINE_WINDOW = 64
ROW_ALIGN = 16


def _combine_kernel(bounds_ref, slot_ref, ye_hbm, x_ref, g_ref, fn_ref, o_ref, stage, onehot, sem, *, nt, cap):
    step = pl.program_id(0)
    w = COMBINE_WINDOW
    cur = step % 2

    def tile_rows(st, e):
        b, i = st // nt, st % nt
        return b, bounds_ref[b * (nt + 1) + i, e], bounds_ref[b * (nt + 1) + i + 1, e]

    def window(st, e, r):
        b, first, _ = tile_rows(st, e)
        base = (first // ROW_ALIGN) * ROW_ALIGN + r * w
        return b, base, jnp.minimum(base, cap - w)

    def window_copy(st, e, r, buf):
        b, _, src = window(st, e, r)
        return pltpu.make_async_copy(ye_hbm.at[e, pl.ds(pl.multiple_of(b * cap + src, ROW_ALIGN), w), :],
                                     stage.at[buf, pl.ds(e * w, w), :], sem.at[buf])

    def start_round(st, r, buf):
        for e in range(N_EXPERTS):
            window_copy(st, e, r, buf).start()

    def wait_round(st, r, buf):
        for e in range(N_EXPERTS):
            window_copy(st, e, r, buf).wait()

    def scatter(r):
        slots = slot_ref[...]
        pos = lax.broadcasted_iota(jnp.int32, (1, w), 1)
        for e in range(N_EXPERTS):
            _, base, src = window(step, e, r)
            col = slots[:, e:e + 1]
            hit = (col >= base) & (col - src == pos)
            onehot[:, e * w:(e + 1) * w] = jnp.where(hit, 1.0, 0.0).astype(BF16)
        return _dot(onehot[...], stage[cur])

    @pl.when(step == 0)
    def _():
        start_round(0, 0, 0)

    @pl.when(step + 1 < pl.num_programs(0))
    def _():
        start_round(step + 1, 0, 1 - cur)

    wait_round(step, 0, cur)
    o_ref[...] = scatter(0)

    rounds = 1
    for e in range(N_EXPERTS):
        _, first, last = tile_rows(step, e)
        rounds = jnp.maximum(rounds, (last - (first // ROW_ALIGN) * ROW_ALIGN + w - 1) // w)

    def extra_round(r, carry):
        start_round(step, r, cur)
        wait_round(step, r, cur)
        o_ref[...] += scatter(r)
        return carry

    lax.fori_loop(1, rounds, extra_round, 0)

    v = x_ref[...] + g_ref[...] * o_ref[...]
    y = v * lax.rsqrt(jnp.mean(v * v, axis=-1, keepdims=True) + NORM_EPS)
    o_ref[...] = y * fn_ref[...]


def _combine(bounds, slot, ye, x2d, mod4, final_norm, batch, t_len, cap, k_gate):
    m, d = x2d.shape
    tm = TOPK_TILE
    nt = t_len // tm
    w = COMBINE_WINDOW
    return pl.pallas_call(
        functools.partial(_combine_kernel, nt=nt, cap=cap),
        out_shape=jax.ShapeDtypeStruct((m, d), F32),
        grid_spec=pltpu.PrefetchScalarGridSpec(
            num_scalar_prefetch=1,
            grid=(batch * nt,),
            in_specs=[pl.BlockSpec((tm, LANES), lambda s, bnd: (s, 0)),
                      pl.BlockSpec(memory_space=pl.ANY),
                      pl.BlockSpec((tm, d), lambda s, bnd: (s, 0)),
                      pl.BlockSpec((None, None, 1, d), lambda s, bnd: (s // nt, k_gate, 0, 0)),
                      pl.BlockSpec((1, d), lambda s, bnd: (0, 0))],
            out_specs=pl.BlockSpec((tm, d), lambda s, bnd: (s, 0)),
            scratch_shapes=[pltpu.VMEM((2, N_EXPERTS * w, d), BF16), pltpu.VMEM((tm, N_EXPERTS * w), BF16),
                            pltpu.SemaphoreType.DMA((2,))]),
        compiler_params=_params("arbitrary"),
        name="combine",
    )(bounds.reshape(batch * (nt + 1), LANES), slot, ye, x2d, mod4, final_norm.reshape(1, d))


def kernel(x, c, ctx, c_ctx, norm1, norm2, w_mod, b_mod, w_in, na_rpb, ret_decay, w_branch_na, w_branch_ret,
           w_out, w_router, w_gate, w_up, w_down, final_norm):
    batch, t_len, d = x.shape
    tc_len = ctx.shape[1]
    na_w = NA_HEADS * NA_HEAD_DIM
    ret_w = RET_HEADS * RET_DIM
    col_qa, col_ka, col_va = 0, na_w, 2 * na_w
    col_qr = 3 * na_w
    col_kr, col_vr, col_gf, col_gb = col_qr + ret_w, col_qr + 2 * ret_w, col_qr + 3 * ret_w, col_qr + 4 * ret_w
    col_ga = col_qr + 5 * ret_w
    col_gb2 = col_ga + d
    cap = EC_CAPACITY_FACTOR * t_len // N_EXPERTS
    assert w_in.shape[0] == 1, "single layer"

    x2d = x.reshape(batch * t_len, d)
    cvec = jnp.concatenate([c, c_ctx[None], jnp.zeros((8 - batch - 1, d), F32)], axis=0)
    mod = _modulation(cvec, w_mod[0], b_mod[0])
    mod4 = mod[:batch + 1].reshape(batch + 1, N_MOD, 1, d)

    h = _prenorm(x2d, norm1[0], mod4, t_len, 0, 0)
    hc = _prenorm(ctx.reshape(batch * tc_len, d), norm1[0], mod4, batch * tc_len, batch, 0)
    tn = 512
    proj = _matmul(h, w_in[0], w_in.shape[2], 2048, tn, lambda j: j, "in_proj")
    kv_w = 2 * na_w
    cproj = _matmul(hc, w_in[0], kv_w + 2 * ret_w, batch * tc_len, tn,
                    lambda j: jnp.where(j < kv_w // tn, col_ka // tn + j, col_kr // tn + j - kv_w // tn), "ctx_proj")

    y_na = _na_attention(proj, cproj, na_rpb[0], batch, t_len, tc_len, col_qa, col_ka, col_va)
    y_ret = _retention(proj, cproj, ret_decay[0], batch, t_len, tc_len, col_qr, col_kr, col_vr, col_gf, col_gb,
                       kv_w, kv_w + ret_w)
    mixed = _merge(y_na, y_ret, w_branch_na[0], w_branch_ret[0], proj, col_ga, col_gb2)
    x1 = _outproj(mixed, w_out[0], x2d, mod4, t_len, 2)

    w_router_pad = jnp.pad(w_router[0], ((0, 0), (0, LANES - N_EXPERTS))).astype(BF16)
    hext, aff = _router(x1, norm2[0], mod4, w_router_pad, t_len, 3)
    slot, idx, bounds = _topk(hext, aff, batch, t_len, d, cap)
    xe, g = _gather(idx, hext, batch, t_len, d, cap)
    act = _expert_up(xe, w_gate[0], w_up[0])
    ye = _expert_down(act, w_down[0], g)
    out = _combine(bounds, slot, ye, x1, mod4, final_norm, batch, t_len, cap, 5)
    return out.reshape(batch, t_len, d)
```

```python
import functools

import numpy as np
import jax
import jax.numpy as jnp
from jax import lax
from jax.experimental import pallas as pl
from jax.experimental.pallas import tpu as pltpu

F32 = jnp.float32
BF16 = jnp.bfloat16

GRID_W = 64
NA_HEADS = 16
NA_HEAD_DIM = 128
NA_WIN_R = 8
NA_WIN_C = 16
RET_HEADS = 8
RET_DIM = 256
RET_CHUNK = 256
N_EXPERTS = 16
EC_CAPACITY_FACTOR = 2
ROPE_BASE = 10000.0
NORM_EPS = 1e-6
NEG_INF = -1e30
N_MOD = 6

VMEM_LIMIT_BYTES = 56 * 1024 * 1024
LANES = 128

NA_Q_ROWS = 4
NA_K_ROWS = NA_Q_ROWS + NA_WIN_R


def _params(*sem):
    return pltpu.CompilerParams(dimension_semantics=sem, vmem_limit_bytes=VMEM_LIMIT_BYTES)


def _dot(a, b):
    return jnp.dot(a, b, preferred_element_type=F32)


def _dot_nt(a, b):
    return lax.dot_general(a, b, (((1,), (1,)), ((), ())), preferred_element_type=F32)


def _dot_tn(a, b):
    return lax.dot_general(a, b, (((0,), (0,)), ((), ())), preferred_element_type=F32)


def _silu(x):
    return x * jax.nn.sigmoid(x)


def _mod_kernel(c_ref, w_ref, b_ref, o_ref):
    a = _silu(c_ref[...]).astype(BF16)
    o_ref[...] = _dot(a, w_ref[...].astype(BF16)) + b_ref[...]


def _modulation(cvec, w_mod, b_mod):
    r, d = cvec.shape
    n = w_mod.shape[1]
    tn = 1024
    return pl.pallas_call(
        _mod_kernel,
        out_shape=jax.ShapeDtypeStruct((r, n), F32),
        grid=(n // tn,),
        in_specs=[pl.BlockSpec((r, d), lambda j: (0, 0)),
                  pl.BlockSpec((d, tn), lambda j: (0, j)),
                  pl.BlockSpec((1, tn), lambda j: (0, j))],
        out_specs=pl.BlockSpec((r, tn), lambda j: (0, j)),
        compiler_params=_params("arbitrary"),
        name="modulation",
    )(cvec, w_mod, b_mod.reshape(1, n))


def _modulated_norm(x, g, shift, scale):
    y = x * lax.rsqrt(jnp.mean(x * x, axis=-1, keepdims=True) + NORM_EPS)
    return y * (g * (1.0 + scale)) + shift


def _prenorm_kernel(x_ref, g_ref, sh_ref, sc_ref, o_ref):
    o_ref[...] = _modulated_norm(x_ref[...], g_ref[...], sh_ref[...], sc_ref[...]).astype(o_ref.dtype)


def _prenorm(x2d, gain, mod4, rows_per_sample, sample0, k_shift):
    r, d = x2d.shape
    tr = 512
    per = rows_per_sample // tr
    return pl.pallas_call(
        _prenorm_kernel,
        out_shape=jax.ShapeDtypeStruct((r, d), BF16),
        grid=(r // tr,),
        in_specs=[pl.BlockSpec((tr, d), lambda i: (i, 0)),
                  pl.BlockSpec((1, d), lambda i: (0, 0)),
                  pl.BlockSpec((None, None, 1, d), lambda i: (sample0 + i // per, k_shift, 0, 0)),
                  pl.BlockSpec((None, None, 1, d), lambda i: (sample0 + i // per, k_shift + 1, 0, 0))],
        out_specs=pl.BlockSpec((tr, d), lambda i: (i, 0)),
        compiler_params=_params("parallel"),
        name="prenorm",
    )(x2d, gain.reshape(1, d), mod4, mod4)


def _mm_kernel(a_ref, w_ref, o_ref):
    o_ref[...] = _dot(a_ref[...], w_ref[...].astype(BF16)).astype(o_ref.dtype)


def _matmul(a, w, n_out, tm, tn, col_block, name):
    m, k = a.shape
    return pl.pallas_call(
        _mm_kernel,
        out_shape=jax.ShapeDtypeStruct((m, n_out), BF16),
        grid=(m // tm, n_out // tn),
        in_specs=[pl.BlockSpec((tm, k), lambda i, j: (i, 0), pipeline_mode=pl.Buffered(1)),
                  pl.BlockSpec((k, tn), lambda i, j: (0, col_block(j)))],
        out_specs=pl.BlockSpec((tm, tn), lambda i, j: (i, j)),
        compiler_params=_params("parallel", "arbitrary"),
        name=name,
    )(a, w)


def _na_tables(rows):
    wr = NA_WIN_R
    bases, tables = [], []
    for t in range(rows // NA_Q_ROWS):
        kb = int(np.clip(NA_Q_ROWS * t - wr // 2, 0, rows - NA_K_ROWS))
        tab = []
        for i in range(NA_Q_ROWS):
            r = NA_Q_ROWS * t + i
            r0 = int(np.clip(r - wr // 2, 0, rows - wr))
            tab.append(tuple((kb + j - r + NA_WIN_R - 1) if r0 <= kb + j < r0 + wr else None
                             for j in range(NA_K_ROWS)))
        bases.append(kb)
        tables.append(tuple(tab))
    uniq = list(dict.fromkeys(tables))
    return bases, uniq, [uniq.index(t) for t in tables]


def _na_bias_diagonals(rpb):
    offset = np.clip(np.arange(LANES) - (GRID_W - 1), -(NA_WIN_C - 1), NA_WIN_C - 1) + (NA_WIN_C - 1)
    return rpb.astype(F32)[:, :, offset]


def _na_build_bias(diag_ref, rowbias_ref, bias_ref, patterns):
    assert LANES == 2 * GRID_W
    shape = (GRID_W, LANES)
    qc = lax.broadcasted_iota(jnp.int32, shape, 0)
    lane = lax.broadcasted_iota(jnp.int32, shape, 1)
    kc = lane % GRID_W
    c0 = jnp.clip(qc - NA_WIN_C // 2, 0, GRID_W - NA_WIN_C)
    col_ok = (kc >= c0) & (kc < c0 + NA_WIN_C)
    low = lane < GRID_W
    for dr in range(diag_ref.shape[0]):
        diag = jnp.broadcast_to(diag_ref[dr:dr + 1, :], shape)
        first = pltpu.roll(diag, GRID_W + 1, 1, stride=1, stride_axis=0)
        second = pltpu.roll(diag, 1, 1, stride=1, stride_axis=0)
        rowbias_ref[dr] = jnp.where(col_ok, jnp.where(low, first, second), NEG_INF)
    outside = jnp.full(shape, NEG_INF, F32)
    block = lambda dr: outside if dr is None else rowbias_ref[dr]
    for p, tab in enumerate(patterns):
        for i, row in enumerate(tab):
            for j in range(0, NA_K_ROWS, 2):
                bias_ref[p, i * GRID_W:(i + 1) * GRID_W, j * GRID_W:(j + 2) * GRID_W] = jnp.where(
                    low, block(row[j]), block(row[j + 1]))


def _na_kernel(q_ref, k_ref, v_ref, kc_ref, vc_ref, diag_ref, o_ref, rowbias_ref, bias_ref, vext_ref, vcext_ref, *,
               rows, patterns, pat_ids):
    tq = NA_Q_ROWS * GRID_W
    nk = NA_K_ROWS * GRID_W
    scale = NA_HEAD_DIM ** -0.5
    for src, dst in ((v_ref, vext_ref), (vc_ref, vcext_ref)):
        dst[:, :NA_HEAD_DIM] = src[...]
        dst[:, NA_HEAD_DIM:] = jnp.ones_like(src)

    @pl.when(pl.program_id(1) == 0)
    def _():
        _na_build_bias(diag_ref, rowbias_ref, bias_ref, patterns)

    def tile(t, carry):
        kb = pl.multiple_of(jnp.clip(NA_Q_ROWS * t - NA_WIN_R // 2, 0, rows - NA_K_ROWS) * GRID_W, tq)
        q0 = pl.multiple_of(t * tq, tq)
        pat = 0
        for i, pid in enumerate(pat_ids):
            pat = jnp.where(t == i, pid, pat)
        q = q_ref[pl.ds(q0, tq), :]
        s_w = _dot_nt(q, k_ref[pl.ds(kb, nk), :]) * scale + bias_ref[pat]
        s_c = _dot_nt(q, kc_ref[...]) * scale
        m = jnp.maximum(jnp.max(s_w, axis=-1, keepdims=True), jnp.max(s_c, axis=-1, keepdims=True))
        p_w = jnp.exp(s_w - m)
        p_c = jnp.exp(s_c - m)
        o = _dot(p_w.astype(BF16), vext_ref[pl.ds(kb, nk), :]) + _dot(p_c.astype(BF16), vcext_ref[...])
        dh = NA_HEAD_DIM
        o_ref[pl.ds(q0, tq), :] = (o[:, :dh] / o[:, dh:dh + 1]).astype(o_ref.dtype)
        return carry

    lax.fori_loop(0, rows // NA_Q_ROWS, tile, 0, unroll=16)


def _na_attention(proj, cproj, rpb, batch, t_len, tc_len, col_q, col_k, col_v):
    rows = t_len // GRID_W
    tq = NA_Q_ROWS * GRID_W
    _, patterns, pat_ids = _na_tables(rows)
    diag = _na_bias_diagonals(rpb)
    n_dr = diag.shape[1]
    dh = NA_HEAD_DIM
    tok = lambda col: pl.BlockSpec((t_len, dh), lambda h, b: (b, col // dh + h))
    return pl.pallas_call(
        functools.partial(_na_kernel, rows=rows, patterns=tuple(patterns), pat_ids=tuple(pat_ids)),
        out_shape=jax.ShapeDtypeStruct((batch * t_len, NA_HEADS * dh), BF16),
        grid=(NA_HEADS, batch),
        in_specs=[tok(col_q), tok(col_k), tok(col_v),
                  pl.BlockSpec((tc_len, dh), lambda h, b: (b, h)),
                  pl.BlockSpec((tc_len, dh), lambda h, b: (b, NA_HEADS + h)),
                  pl.BlockSpec((None, n_dr, LANES), lambda h, b: (h, 0, 0))],
        out_specs=pl.BlockSpec((t_len, dh), lambda h, b: (b, h)),
        scratch_shapes=[pltpu.VMEM((n_dr, GRID_W, LANES), F32),
                        pltpu.VMEM((len(patterns), tq, NA_K_ROWS * GRID_W), F32),
                        pltpu.VMEM((t_len, 2 * dh), BF16), pltpu.VMEM((tc_len, 2 * dh), BF16)],
        compiler_params=_params("arbitrary", "arbitrary"),
        name="na_attention",
    )(proj, proj, proj, cproj, cproj, diag)


def _rope_tables(t_len):
    quarter = RET_DIM // 4
    inv = ROPE_BASE ** (-np.arange(quarter, dtype=np.float64) / quarter)
    tpos = np.arange(t_len)
    row_ang = (tpos // GRID_W).astype(np.float64)[:, None] * inv[None, :]
    col_ang = (tpos % GRID_W).astype(np.float64)[:, None] * inv[None, :]
    cos = np.concatenate([np.cos(row_ang)] * 2 + [np.cos(col_ang)] * 2, axis=-1)
    sin = np.concatenate([-np.sin(row_ang), np.sin(row_ang), -np.sin(col_ang), np.sin(col_ang)], axis=-1)
    return jnp.asarray(cos, F32), jnp.asarray(sin, F32)


def _rope(a, cos, sin):
    half = RET_DIM // 2
    swapped = jnp.concatenate([pltpu.roll(a[:, :half], half // 2, 1), pltpu.roll(a[:, half:], half // 2, 1)], axis=1)
    return a * cos + swapped * sin


def _ret_kernel(dec_ref, q_ref, k_ref, v_ref, gf_ref, gb_ref, kc_ref, vc_ref, cos_ref, sin_ref, o_ref,
                sf_ref, sb_ref, acc_ref, qrot_ref, krot_ref):
    h = pl.program_id(1)
    c = RET_CHUNK
    t_len = q_ref.shape[0]
    tc_len = kc_ref.shape[0]
    nc = t_len // c
    k_scale = RET_DIM ** -0.5

    def log_gamma(direction):
        e = jnp.full((1, 1), dec_ref[direction, h], F32)
        return jnp.log1p(-jnp.exp2(-e))

    lg_f, lg_b = log_gamma(0), log_gamma(1)
    pos = lax.broadcasted_iota(jnp.int32, (c, 1), 0).astype(F32)
    diff = pos - lax.broadcasted_iota(jnp.int32, (1, c), 1).astype(F32)
    dec_f = jnp.where(diff >= 0, jnp.exp(lg_f * jnp.maximum(diff, 0.0)), 0.0) * k_scale
    dec_b = jnp.where(diff <= 0, jnp.exp(lg_b * jnp.maximum(-diff, 0.0)), 0.0) * k_scale
    qdec_f, kdec_f, cdec_f = jnp.exp(lg_f * (pos + 1.0)), jnp.exp(lg_f * (c - 1.0 - pos)) * k_scale, jnp.exp(lg_f * c)
    qdec_b, kdec_b, cdec_b = jnp.exp(lg_b * (c - pos)), jnp.exp(lg_b * pos) * k_scale, jnp.exp(lg_b * c)

    cpos = lax.broadcasted_iota(jnp.int32, (tc_len, 1), 0).astype(F32)
    kc = kc_ref[...].astype(F32) * k_scale
    vc = vc_ref[...]
    sf_ref[...] = _dot_tn((kc * jnp.exp(lg_f * (tc_len - 1.0 - cpos))).astype(BF16), vc)
    sb_ref[...] = _dot_tn((kc * jnp.exp(lg_b * cpos)).astype(BF16), vc)

    def chunk(n, s_ref, dec, qdec, kdec, cdec, g_ref, first_visit):
        rows = pl.ds(pl.multiple_of(n * c, c), c)
        if first_visit:
            cos, sin = cos_ref[rows, :], sin_ref[rows, :]
            qb = _rope(q_ref[rows, :].astype(F32), cos, sin).astype(BF16)
            k = _rope(k_ref[rows, :].astype(F32), cos, sin)
            qrot_ref[rows, :] = qb
            krot_ref[rows, :] = k
        else:
            qb, k = qrot_ref[rows, :], krot_ref[rows, :]
        v = v_ref[rows, :]
        scores = _dot_nt(qb, k.astype(BF16)) * dec
        s = s_ref[...]
        o = _dot(scores.astype(BF16), v) + _dot(qb, s.astype(BF16)) * qdec
        s_ref[...] = s * cdec + _dot_tn((k * kdec).astype(BF16), v)
        on = o * lax.rsqrt(jnp.mean(o * o, axis=-1, keepdims=True) + NORM_EPS)
        gated = _silu(g_ref[rows, :].astype(F32)) * on
        if first_visit:
            acc_ref[rows, :] = gated
        else:
            acc_ref[rows, :] += gated

    assert nc % 2 == 0

    def steps(first_visit):
        def body(n, carry):
            chunk(n, sf_ref, dec_f, qdec_f, kdec_f, cdec_f, gf_ref, first_visit)
            chunk(nc - 1 - n, sb_ref, dec_b, qdec_b, kdec_b, cdec_b, gb_ref, first_visit)
            return carry
        return body

    lax.fori_loop(0, nc // 2, steps(True), 0, unroll=2)
    lax.fori_loop(nc // 2, nc, steps(False), 0, unroll=8)
    o_ref[...] = acc_ref[...].astype(o_ref.dtype)


def _retention(proj, cproj, ret_decay, batch, t_len, tc_len, col_q, col_k, col_v, col_gf, col_gb, ccol_k, ccol_v):
    d = RET_DIM
    cos, sin = _rope_tables(t_len)
    tok = lambda col: pl.BlockSpec((t_len, d), lambda b, h: (b, col // d + h))
    ctx = lambda col: pl.BlockSpec((tc_len, d), lambda b, h: (b, col // d + h))
    tab = pl.BlockSpec((t_len, d), lambda b, h: (0, 0))
    return pl.pallas_call(
        _ret_kernel,
        out_shape=jax.ShapeDtypeStruct((batch * t_len, RET_HEADS * d), BF16),
        grid=(batch, RET_HEADS),
        in_specs=[pl.BlockSpec(memory_space=pltpu.SMEM),
                  tok(col_q), tok(col_k), tok(col_v), tok(col_gf), tok(col_gb), ctx(ccol_k), ctx(ccol_v), tab, tab],
        out_specs=pl.BlockSpec((t_len, d), lambda b, h: (b, h)),
        scratch_shapes=[pltpu.VMEM((d, d), F32), pltpu.VMEM((d, d), F32), pltpu.VMEM((t_len, d), F32),
                        pltpu.VMEM((t_len, d), BF16), pltpu.VMEM((t_len, d), F32)],
        compiler_params=_params("parallel", "arbitrary"),
        name="retention",
    )(ret_decay.astype(F32), proj, proj, proj, proj, proj, cproj, cproj, cos, sin)


def _merge_kernel(ya_ref, yr_ref, wa_ref, wr_ref, ga_ref, gb_ref, o_ref):
    a = _dot(ya_ref[...], wa_ref[...].astype(BF16))
    r = _dot(yr_ref[...], wr_ref[...].astype(BF16))
    o_ref[...] = (jax.nn.sigmoid(ga_ref[...].astype(F32)) * a + jax.nn.sigmoid(gb_ref[...].astype(F32)) * r
                  ).astype(o_ref.dtype)


def _merge(y_na, y_ret, w_na, w_ret, proj, col_ga, col_gb):
    m, ka = y_na.shape
    kr = y_ret.shape[1]
    n = w_na.shape[1]
    tm, tn = 1024, 512
    return pl.pallas_call(
        _merge_kernel,
        out_shape=jax.ShapeDtypeStruct((m, n), BF16),
        grid=(m // tm, n // tn),
        in_specs=[pl.BlockSpec((tm, ka), lambda i, j: (i, 0)),
                  pl.BlockSpec((tm, kr), lambda i, j: (i, 0)),
                  pl.BlockSpec((ka, tn), lambda i, j: (0, j)),
                  pl.BlockSpec((kr, tn), lambda i, j: (0, j)),
                  pl.BlockSpec((tm, tn), lambda i, j: (i, col_ga // tn + j)),
                  pl.BlockSpec((tm, tn), lambda i, j: (i, col_gb // tn + j))],
        out_specs=pl.BlockSpec((tm, tn), lambda i, j: (i, j)),
        compiler_params=_params("parallel", "arbitrary"),
        name="merge",
    )(y_na, y_ret, w_na, w_ret, proj, proj)


def _outproj_kernel(m_ref, w_ref, x_ref, g_ref, o_ref):
    o_ref[...] = x_ref[...] + g_ref[...] * _dot(m_ref[...], w_ref[...].astype(BF16))


def _outproj(mixed, w_out, x2d, mod4, rows_per_sample, k_gate):
    m, k = mixed.shape
    n = w_out.shape[1]
    tm, tn = 1024, 512
    per = rows_per_sample // tm
    return pl.pallas_call(
        _outproj_kernel,
        out_shape=jax.ShapeDtypeStruct((m, n), F32),
        grid=(m // tm, n // tn),
        in_specs=[pl.BlockSpec((tm, k), lambda i, j: (i, 0)),
                  pl.BlockSpec((k, tn), lambda i, j: (0, j)),
                  pl.BlockSpec((tm, tn), lambda i, j: (i, j)),
                  pl.BlockSpec((None, None, 1, tn), lambda i, j: (i // per, k_gate, 0, j))],
        out_specs=pl.BlockSpec((tm, tn), lambda i, j: (i, j)),
        compiler_params=_params("parallel", "arbitrary"),
        name="outproj",
    )(mixed, w_out, x2d, mod4)


def _router_kernel(x_ref, g_ref, sh_ref, sc_ref, wr_ref, o_ref, a_ref):
    d = x_ref.shape[1]
    h = _modulated_norm(x_ref[...], g_ref[...], sh_ref[...], sc_ref[...])
    logits = _dot(h.astype(BF16), wr_ref[...])
    lane = lax.broadcasted_iota(jnp.int32, logits.shape, 1)
    logits = jnp.where(lane < N_EXPERTS, logits, NEG_INF)
    p = jnp.exp(logits - jnp.max(logits, axis=-1, keepdims=True))
    aff = p / jnp.sum(p, axis=-1, keepdims=True)
    o_ref[:, :d] = h
    o_ref[:, d:] = aff
    a_ref[...] = aff[:, :N_EXPERTS]


def _router(x2d, gain, mod4, w_router_pad, rows_per_sample, k_shift):
    r, d = x2d.shape
    tr = 512
    per = rows_per_sample // tr
    return pl.pallas_call(
        _router_kernel,
        out_shape=(jax.ShapeDtypeStruct((r, d + LANES), F32), jax.ShapeDtypeStruct((r, N_EXPERTS), F32)),
        grid=(r // tr,),
        in_specs=[pl.BlockSpec((tr, d), lambda i: (i, 0)),
                  pl.BlockSpec((1, d), lambda i: (0, 0)),
                  pl.BlockSpec((None, None, 1, d), lambda i: (i // per, k_shift, 0, 0)),
                  pl.BlockSpec((None, None, 1, d), lambda i: (i // per, k_shift + 1, 0, 0)),
                  pl.BlockSpec((d, LANES), lambda i: (0, 0))],
        out_specs=(pl.BlockSpec((tr, d + LANES), lambda i: (i, 0)), pl.BlockSpec((tr, N_EXPERTS), lambda i: (i, 0))),
        compiler_params=_params("parallel"),
        name="router",
    )(x2d, gain.reshape(1, d), mod4, mod4, w_router_pad)


TOPK_TILE = 256
GEOMETRIC_STEPS = 32
ARITHMETIC_STEPS = 12


def _topk_kernel(aff_ref, affc_ref, slot_ref, idx_ref, bounds_ref, slot_t_ref, *, cap):
    t_len = aff_ref.shape[0]
    tt = TOPK_TILE
    nt = t_len // tt
    packed = affc_ref[...]

    def per_expert(v):
        shift = LANES // 2
        while shift >= N_EXPERTS:
            v = v + pltpu.roll(v, shift, 1)
            shift //= 2
        return v

    def narrow(c, mid):
        lo, hi = c
        ge = per_expert(jnp.sum(jnp.where(packed >= mid, 1.0, 0.0), axis=0, keepdims=True)) >= cap
        return jnp.where(ge, mid, lo), jnp.where(ge, hi, mid)

    tiny = float(np.finfo(np.float32).tiny)
    above_tiny = per_expert(jnp.sum(jnp.where(packed >= tiny, 1.0, 0.0), axis=0, keepdims=True)) >= cap
    bracket = (jnp.where(above_tiny, tiny, 0.0), jnp.where(above_tiny, 2.0, tiny) + jnp.zeros((1, LANES), F32))
    geometric_mid = lambda c: jnp.clip(jnp.sqrt(c[0]) * jnp.sqrt(c[1]), c[0], c[1])
    bracket = lax.fori_loop(0, GEOMETRIC_STEPS, lambda _, c: narrow(c, geometric_mid(c)), bracket)
    lo, hi = lax.fori_loop(0, ARITHMETIC_STEPS, lambda _, c: narrow(c, 0.5 * (c[0] + c[1])), bracket)

    def count_above(i, cnt):
        r0 = pl.multiple_of(i * tt, tt)
        return cnt + jnp.sum(jnp.where(aff_ref[pl.ds(r0, tt), :] >= hi, 1.0, 0.0), axis=0, keepdims=True)
    need = cap - lax.fori_loop(0, nt, count_above, jnp.zeros((1, LANES), F32))

    tri = jnp.where(lax.broadcasted_iota(jnp.int32, (tt, tt), 0) >= lax.broadcasted_iota(jnp.int32, (tt, tt), 1),
                    1.0, 0.0).astype(BF16)

    def assign(i, carry):
        eq_before, sel_before = carry
        r0 = pl.multiple_of(i * tt, tt)
        a = aff_ref[pl.ds(r0, tt), :]
        above = a >= hi
        tie = (a >= lo) & (a < hi)
        eq = jnp.where(tie, 1.0, 0.0)
        eq_rank = _dot(tri, eq.astype(BF16)) + eq_before
        sel = jnp.where(above | (tie & (eq_rank <= need)), 1.0, 0.0)
        sel_rank = _dot(tri, sel.astype(BF16)) + sel_before
        slot_ref[pl.ds(r0, tt), :] = jnp.where(sel > 0, sel_rank - 1.0, -1.0).astype(jnp.int32)
        bounds_ref[pl.ds(i, 1), :] = sel_before.astype(jnp.int32)
        return (eq_before + jnp.sum(eq, axis=0, keepdims=True), sel_before + jnp.sum(sel, axis=0, keepdims=True))

    zero = jnp.zeros((1, LANES), F32)
    _, total = lax.fori_loop(0, nt, assign, (zero, zero), unroll=2)
    bounds_ref[nt:nt + 1, :] = total.astype(jnp.int32)

    slot_t_ref[...] = jnp.transpose(slot_ref[...].astype(F32))
    idx_ref[...] = jnp.zeros_like(idx_ref)
    sublanes = 8
    tok = lax.broadcasted_iota(jnp.int32, (sublanes, t_len), 1).astype(F32)
    sub = lax.broadcasted_iota(jnp.int32, (sublanes, 1), 0).astype(F32)
    for e in range(N_EXPERTS):
        def body(g, carry):
            s0 = pl.multiple_of(g * sublanes, sublanes)
            hit = slot_t_ref[e:e + 1, :] == sub + s0.astype(F32)
            idx_ref[pl.ds(s0, sublanes), e:e + 1] = jnp.sum(jnp.where(hit, tok, 0.0), axis=1,
                                                            keepdims=True).astype(jnp.int32)
            return carry
        lax.fori_loop(0, cap // sublanes, body, 0, unroll=8)


def _topk(hext, aff, batch, t_len, d, cap):
    nb = t_len // TOPK_TILE + 1
    packed_rows = t_len * N_EXPERTS // LANES
    slot, idx_t, bounds = pl.pallas_call(
        functools.partial(_topk_kernel, cap=cap),
        out_shape=(jax.ShapeDtypeStruct((batch * t_len, LANES), jnp.int32),
                   jax.ShapeDtypeStruct((batch, cap, LANES), jnp.int32),
                   jax.ShapeDtypeStruct((batch, nb, LANES), jnp.int32)),
        grid=(batch,),
        in_specs=[pl.BlockSpec((t_len, LANES), lambda b: (b, d // LANES)),
                  pl.BlockSpec((None, packed_rows, LANES), lambda b: (b, 0, 0))],
        out_specs=(pl.BlockSpec((t_len, LANES), lambda b: (b, 0)),
                   pl.BlockSpec((None, cap, LANES), lambda b: (b, 0, 0)),
                   pl.BlockSpec((None, nb, LANES), lambda b: (b, 0, 0))),
        scratch_shapes=[pltpu.VMEM((LANES, t_len), F32)],
        compiler_params=_params("parallel"),
        name="topk",
    )(hext, aff.reshape(batch, packed_rows, LANES))
    return slot, idx_t[:, :, :N_EXPERTS].transpose(0, 2, 1), bounds


GATHER_ROWS = 256


def _gather_kernel(idx_ref, h_hbm, xe_ref, g_ref, buf, sem, *, t_len, d, nchunk):
    step = pl.program_id(0)
    rc = xe_ref.shape[0]
    cur = step % 2

    def request(st, slot):
        lst, chunk = st // nchunk, st % nchunk
        row0 = (lst // N_EXPERTS) * t_len

        for r in range(rc):
            row = row0 + idx_ref[lst, chunk * rc + r]
            pltpu.make_async_copy(h_hbm.at[pl.ds(row, 1)], buf.at[slot, pl.ds(r, 1)], sem.at[slot]).start(
                priority=r % 2)

    @pl.when(step == 0)
    def _():
        request(0, 0)

    @pl.when(step + 1 < pl.num_programs(0))
    def _():
        request(step + 1, 1 - cur)

    pltpu.make_async_copy(h_hbm.at[pl.ds(0, rc)], buf.at[cur], sem.at[cur]).wait()
    rows = buf[cur]
    xe_ref[...] = rows[:, :d].astype(xe_ref.dtype)
    aff = rows[:, d:]
    e = (step // nchunk) % N_EXPERTS
    lane = lax.broadcasted_iota(jnp.int32, aff.shape, 1)
    g_ref[...] = jnp.broadcast_to(jnp.sum(jnp.where(lane == e, aff, 0.0), axis=1, keepdims=True), aff.shape)


def _gather(idx, hext, batch, t_len, d, cap):
    rc = min(GATHER_ROWS, cap)
    nchunk = cap // rc

    def out_block(s, idx):
        lst = s // nchunk
        return lst % N_EXPERTS, (lst // N_EXPERTS) * nchunk + s % nchunk, 0

    return pl.pallas_call(
        functools.partial(_gather_kernel, t_len=t_len, d=d, nchunk=nchunk),
        out_shape=(jax.ShapeDtypeStruct((N_EXPERTS, batch * cap, d), BF16),
                   jax.ShapeDtypeStruct((N_EXPERTS, batch * cap, LANES), F32)),
        grid_spec=pltpu.PrefetchScalarGridSpec(
            num_scalar_prefetch=1,
            grid=(batch * N_EXPERTS * nchunk,),
            in_specs=[pl.BlockSpec(memory_space=pl.ANY)],
            out_specs=(pl.BlockSpec((None, rc, d), out_block), pl.BlockSpec((None, rc, LANES), out_block)),
            scratch_shapes=[pltpu.VMEM((2, rc, d + LANES), F32), pltpu.SemaphoreType.DMA((2,))]),
        compiler_params=_params("arbitrary"),
        name="gather",
    )(idx.reshape(batch * N_EXPERTS, cap), hext)


def _expert_up_kernel(x_ref, wg_ref, wu_ref, o_ref):
    x = x_ref[...]
    a = _dot(x, wg_ref[...].astype(BF16))
    u = _dot(x, wu_ref[...].astype(BF16))
    o_ref[...] = (_silu(a) * u).astype(o_ref.dtype)


def _expert_up(xe, w_gate, w_up):
    e, m, d = xe.shape
    ff = w_gate.shape[2]
    tf = 256
    return pl.pallas_call(
        _expert_up_kernel,
        out_shape=jax.ShapeDtypeStruct((e, m, ff), BF16),
        grid=(e, ff // tf),
        in_specs=[pl.BlockSpec((None, m, d), lambda i, f: (i, 0, 0)),
                  pl.BlockSpec((None, d, tf), lambda i, f: (i, 0, f)),
                  pl.BlockSpec((None, d, tf), lambda i, f: (i, 0, f))],
        out_specs=pl.BlockSpec((None, m, tf), lambda i, f: (i, 0, f)),
        compiler_params=_params("parallel", "arbitrary"),
        name="expert_up",
    )(xe, w_gate, w_up)


def _expert_down_kernel(a_ref, w_ref, g_ref, o_ref):
    o_ref[...] = (_dot(a_ref[...], w_ref[...].astype(BF16)) * g_ref[:, :1]).astype(o_ref.dtype)


def _expert_down(act, w_down, g):
    e, m, ff = act.shape
    d = w_down.shape[2]
    tn = min(1024, d)
    return pl.pallas_call(
        _expert_down_kernel,
        out_shape=jax.ShapeDtypeStruct((e, m, d), BF16),
        grid=(e, d // tn),
        in_specs=[pl.BlockSpec((None, m, ff), lambda i, j: (i, 0, 0)),
                  pl.BlockSpec((None, ff, tn), lambda i, j: (i, 0, j)),
                  pl.BlockSpec((None, m, LANES), lambda i, j: (i, 0, 0))],
        out_specs=pl.BlockSpec((None, m, tn), lambda i, j: (i, 0, j)),
        compiler_params=_params("parallel", "arbitrary"),
        name="expert_down",
    )(act, w_down, g)


COMBINE_WINDOW = 64
ROW_ALIGN = 16


def _combine_kernel(bounds_ref, slot_ref, ye_hbm, x_ref, g_ref, fn_ref, o_ref, stage, onehot, sem, *, nt, cap):
    step = pl.program_id(0)
    w = COMBINE_WINDOW
    cur = step % 2

    def tile_rows(st, e):
        b, i = st // nt, st % nt
        return b, bounds_ref[b * (nt + 1) + i, e], bounds_ref[b * (nt + 1) + i + 1, e]

    def window(st, e, r):
        b, first, _ = tile_rows(st, e)
        base = (first // ROW_ALIGN) * ROW_ALIGN + r * w
        return b, base, jnp.minimum(base, cap - w)

    def window_copy(st, e, r, buf):
        b, _, src = window(st, e, r)
        return pltpu.make_async_copy(ye_hbm.at[e, pl.ds(pl.multiple_of(b * cap + src, ROW_ALIGN), w), :],
                                     stage.at[buf, pl.ds(e * w, w), :], sem.at[buf])

    def start_round(st, r, buf):
        for e in range(N_EXPERTS):
            window_copy(st, e, r, buf).start()

    def wait_round(st, r, buf):
        for e in range(N_EXPERTS):
            window_copy(st, e, r, buf).wait()

    def scatter(r):
        slots = slot_ref[...]
        pos = lax.broadcasted_iota(jnp.int32, (1, w), 1)
        for e in range(N_EXPERTS):
            _, base, src = window(step, e, r)
            col = slots[:, e:e + 1]
            hit = (col >= base) & (col - src == pos)
            onehot[:, e * w:(e + 1) * w] = jnp.where(hit, 1.0, 0.0).astype(BF16)
        return _dot(onehot[...], stage[cur])

    @pl.when(step == 0)
    def _():
        start_round(0, 0, 0)

    @pl.when(step + 1 < pl.num_programs(0))
    def _():
        start_round(step + 1, 0, 1 - cur)

    wait_round(step, 0, cur)
    o_ref[...] = scatter(0)

    rounds = 1
    for e in range(N_EXPERTS):
        _, first, last = tile_rows(step, e)
        rounds = jnp.maximum(rounds, (last - (first // ROW_ALIGN) * ROW_ALIGN + w - 1) // w)

    def extra_round(r, carry):
        start_round(step, r, cur)
        wait_round(step, r, cur)
        o_ref[...] += scatter(r)
        return carry

    lax.fori_loop(1, rounds, extra_round, 0)

    v = x_ref[...] + g_ref[...] * o_ref[...]
    y = v * lax.rsqrt(jnp.mean(v * v, axis=-1, keepdims=True) + NORM_EPS)
    o_ref[...] = y * fn_ref[...]


def _combine(bounds, slot, ye, x2d, mod4, final_norm, batch, t_len, cap, k_gate):
    m, d = x2d.shape
    tm = TOPK_TILE
    nt = t_len // tm
    w = COMBINE_WINDOW
    return pl.pallas_call(
        functools.partial(_combine_kernel, nt=nt, cap=cap),
        out_shape=jax.ShapeDtypeStruct((m, d), F32),
        grid_spec=pltpu.PrefetchScalarGridSpec(
            num_scalar_prefetch=1,
            grid=(batch * nt,),
            in_specs=[pl.BlockSpec((tm, LANES), lambda s, bnd: (s, 0)),
                      pl.BlockSpec(memory_space=pl.ANY),
                      pl.BlockSpec((tm, d), lambda s, bnd: (s, 0)),
                      pl.BlockSpec((None, None, 1, d), lambda s, bnd: (s // nt, k_gate, 0, 0)),
                      pl.BlockSpec((1, d), lambda s, bnd: (0, 0))],
            out_specs=pl.BlockSpec((tm, d), lambda s, bnd: (s, 0)),
            scratch_shapes=[pltpu.VMEM((2, N_EXPERTS * w, d), BF16), pltpu.VMEM((tm, N_EXPERTS * w), BF16),
                            pltpu.SemaphoreType.DMA((2,))]),
        compiler_params=_params("arbitrary"),
        name="combine",
    )(bounds.reshape(batch * (nt + 1), LANES), slot, ye, x2d, mod4, final_norm.reshape(1, d))


def kernel(x, c, ctx, c_ctx, norm1, norm2, w_mod, b_mod, w_in, na_rpb, ret_decay, w_branch_na, w_branch_ret,
           w_out, w_router, w_gate, w_up, w_down, final_norm):
    batch, t_len, d = x.shape
    tc_len = ctx.shape[1]
    na_w = NA_HEADS * NA_HEAD_DIM
    ret_w = RET_HEADS * RET_DIM
    col_qa, col_ka, col_va = 0, na_w, 2 * na_w
    col_qr = 3 * na_w
    col_kr, col_vr, col_gf, col_gb = col_qr + ret_w, col_qr + 2 * ret_w, col_qr + 3 * ret_w, col_qr + 4 * ret_w
    col_ga = col_qr + 5 * ret_w
    col_gb2 = col_ga + d
    cap = EC_CAPACITY_FACTOR * t_len // N_EXPERTS
    assert w_in.shape[0] == 1, "single layer"

    x2d = x.reshape(batch * t_len, d)
    cvec = jnp.concatenate([c, c_ctx[None], jnp.zeros((8 - batch - 1, d), F32)], axis=0)
    mod = _modulation(cvec, w_mod[0], b_mod[0])
    mod4 = mod[:batch + 1].reshape(batch + 1, N_MOD, 1, d)

    h = _prenorm(x2d, norm1[0], mod4, t_len, 0, 0)
    hc = _prenorm(ctx.reshape(batch * tc_len, d), norm1[0], mod4, batch * tc_len, batch, 0)
    tn = 512
    proj = _matmul(h, w_in[0], w_in.shape[2], 2048, tn, lambda j: j, "in_proj")
    kv_w = 2 * na_w
    cproj = _matmul(hc, w_in[0], kv_w + 2 * ret_w, batch * tc_len, tn,
                    lambda j: jnp.where(j < kv_w // tn, col_ka // tn + j, col_kr // tn + j - kv_w // tn), "ctx_proj")

    y_na = _na_attention(proj, cproj, na_rpb[0], batch, t_len, tc_len, col_qa, col_ka, col_va)
    y_ret = _retention(proj, cproj, ret_decay[0], batch, t_len, tc_len, col_qr, col_kr, col_vr, col_gf, col_gb,
                       kv_w, kv_w + ret_w)
    mixed = _merge(y_na, y_ret, w_branch_na[0], w_branch_ret[0], proj, col_ga, col_gb2)
    x1 = _outproj(mixed, w_out[0], x2d, mod4, t_len, 2)

    w_router_pad = jnp.pad(w_router[0], ((0, 0), (0, LANES - N_EXPERTS))).astype(BF16)
    hext, aff = _router(x1, norm2[0], mod4, w_router_pad, t_len, 3)
    slot, idx, bounds = _topk(hext, aff, batch, t_len, d, cap)
    xe, g = _gather(idx, hext, batch, t_len, d, cap)
    act = _expert_up(xe, w_gate[0], w_up[0])
    ye = _expert_down(act, w_down[0], g)
    out = _combine(bounds, slot, ye, x1, mod4, final_norm, batch, t_len, cap, 5)
    return out.reshape(batch, t_len, d)
```

```python
import functools

import numpy as np
import jax
import jax.numpy as jnp
from jax import lax
from jax.experimental import pallas as pl
from jax.experimental.pallas import tpu as pltpu

F32 = jnp.float32
BF16 = jnp.bfloat16

GRID_W = 64
NA_HEADS = 16
NA_HEAD_DIM = 128
NA_WIN_R = 8
NA_WIN_C = 16
RET_HEADS = 8
RET_DIM = 256
RET_CHUNK = 256
N_EXPERTS = 16
EC_CAPACITY_FACTOR = 2
ROPE_BASE = 10000.0
NORM_EPS = 1e-6
NEG_INF = -1e30
N_MOD = 6

VMEM_LIMIT_BYTES = 56 * 1024 * 1024
LANES = 128

NA_Q_ROWS = 4
NA_K_ROWS = NA_Q_ROWS + NA_WIN_R


def _params(*sem):
    return pltpu.CompilerParams(dimension_semantics=sem, vmem_limit_bytes=VMEM_LIMIT_BYTES)


def _dot(a, b):
    return jnp.dot(a, b, preferred_element_type=F32)


def _dot_nt(a, b):
    return lax.dot_general(a, b, (((1,), (1,)), ((), ())), preferred_element_type=F32)


def _dot_tn(a, b):
    return lax.dot_general(a, b, (((0,), (0,)), ((), ())), preferred_element_type=F32)


def _silu(x):
    return x * jax.nn.sigmoid(x)


def _mod_kernel(c_ref, w_ref, b_ref, o_ref):
    a = _silu(c_ref[...]).astype(BF16)
    o_ref[...] = _dot(a, w_ref[...].astype(BF16)) + b_ref[...]


def _modulation(cvec, w_mod, b_mod):
    r, d = cvec.shape
    n = w_mod.shape[1]
    tn = 1024
    return pl.pallas_call(
        _mod_kernel,
        out_shape=jax.ShapeDtypeStruct((r, n), F32),
        grid=(n // tn,),
        in_specs=[pl.BlockSpec((r, d), lambda j: (0, 0)),
                  pl.BlockSpec((d, tn), lambda j: (0, j)),
                  pl.BlockSpec((1, tn), lambda j: (0, j))],
        out_specs=pl.BlockSpec((r, tn), lambda j: (0, j)),
        compiler_params=_params("arbitrary"),
        name="modulation",
    )(cvec, w_mod, b_mod.reshape(1, n))


def _modulated_norm(x, g, shift, scale):
    y = x * lax.rsqrt(jnp.mean(x * x, axis=-1, keepdims=True) + NORM_EPS)
    return (y * g) * (1.0 + scale) + shift


def _prenorm_kernel(x_ref, g_ref, sh_ref, sc_ref, o_ref):
    o_ref[...] = _modulated_norm(x_ref[...], g_ref[...], sh_ref[...], sc_ref[...]).astype(o_ref.dtype)


def _prenorm(x2d, gain, mod4, rows_per_sample, sample0, k_shift):
    r, d = x2d.shape
    tr = 512
    per = rows_per_sample // tr
    return pl.pallas_call(
        _prenorm_kernel,
        out_shape=jax.ShapeDtypeStruct((r, d), BF16),
        grid=(r // tr,),
        in_specs=[pl.BlockSpec((tr, d), lambda i: (i, 0)),
                  pl.BlockSpec((1, d), lambda i: (0, 0)),
                  pl.BlockSpec((None, None, 1, d), lambda i: (sample0 + i // per, k_shift, 0, 0)),
                  pl.BlockSpec((None, None, 1, d), lambda i: (sample0 + i // per, k_shift + 1, 0, 0))],
        out_specs=pl.BlockSpec((tr, d), lambda i: (i, 0)),
        compiler_params=_params("parallel"),
        name="prenorm",
    )(x2d, gain.reshape(1, d), mod4, mod4)


def _mm_kernel(a_ref, w_ref, o_ref):
    o_ref[...] = _dot(a_ref[...], w_ref[...].astype(BF16)).astype(o_ref.dtype)


def _matmul(a, w, n_out, tm, tn, col_block, name):
    m, k = a.shape
    return pl.pallas_call(
        _mm_kernel,
        out_shape=jax.ShapeDtypeStruct((m, n_out), BF16),
        grid=(m // tm, n_out // tn),
        in_specs=[pl.BlockSpec((tm, k), lambda i, j: (i, 0), pipeline_mode=pl.Buffered(1)),
                  pl.BlockSpec((k, tn), lambda i, j: (0, col_block(j)))],
        out_specs=pl.BlockSpec((tm, tn), lambda i, j: (i, j)),
        compiler_params=_params("parallel", "arbitrary"),
        name=name,
    )(a, w)


def _na_tables(rows):
    wr = NA_WIN_R
    bases, tables = [], []
    for t in range(rows // NA_Q_ROWS):
        kb = int(np.clip(NA_Q_ROWS * t - wr // 2, 0, rows - NA_K_ROWS))
        tab = []
        for i in range(NA_Q_ROWS):
            r = NA_Q_ROWS * t + i
            r0 = int(np.clip(r - wr // 2, 0, rows - wr))
            tab.append(tuple((kb + j - r + NA_WIN_R - 1) if r0 <= kb + j < r0 + wr else None
                             for j in range(NA_K_ROWS)))
        bases.append(kb)
        tables.append(tuple(tab))
    uniq = list(dict.fromkeys(tables))
    return bases, uniq, [uniq.index(t) for t in tables]


def _na_bias_diagonals(rpb):
    offset = np.clip(np.arange(LANES) - (GRID_W - 1), -(NA_WIN_C - 1), NA_WIN_C - 1) + (NA_WIN_C - 1)
    return rpb.astype(F32)[:, :, offset]


def _na_build_bias(diag_ref, rowbias_ref, bias_ref, patterns):
    assert LANES == 2 * GRID_W
    shape = (GRID_W, LANES)
    qc = lax.broadcasted_iota(jnp.int32, shape, 0)
    lane = lax.broadcasted_iota(jnp.int32, shape, 1)
    kc = lane % GRID_W
    c0 = jnp.clip(qc - NA_WIN_C // 2, 0, GRID_W - NA_WIN_C)
    col_ok = (kc >= c0) & (kc < c0 + NA_WIN_C)
    low = lane < GRID_W
    for dr in range(diag_ref.shape[0]):
        diag = jnp.broadcast_to(diag_ref[dr:dr + 1, :], shape)
        first = pltpu.roll(diag, GRID_W + 1, 1, stride=1, stride_axis=0)
        second = pltpu.roll(diag, 1, 1, stride=1, stride_axis=0)
        rowbias_ref[dr] = jnp.where(col_ok, jnp.where(low, first, second), NEG_INF)
    outside = jnp.full(shape, NEG_INF, F32)
    block = lambda dr: outside if dr is None else rowbias_ref[dr]
    for p, tab in enumerate(patterns):
        for i, row in enumerate(tab):
            for j in range(0, NA_K_ROWS, 2):
                bias_ref[p, i * GRID_W:(i + 1) * GRID_W, j * GRID_W:(j + 2) * GRID_W] = jnp.where(
                    low, block(row[j]), block(row[j + 1]))


def _na_kernel(q_ref, k_ref, v_ref, kc_ref, vc_ref, diag_ref, o_ref, rowbias_ref, bias_ref, vext_ref, vcext_ref, *,
               rows, patterns, pat_ids):
    tq = NA_Q_ROWS * GRID_W
    nk = NA_K_ROWS * GRID_W
    scale = NA_HEAD_DIM ** -0.5
    for src, dst in ((v_ref, vext_ref), (vc_ref, vcext_ref)):
        dst[:, :NA_HEAD_DIM] = src[...]
        dst[:, NA_HEAD_DIM:] = jnp.ones_like(src)

    @pl.when(pl.program_id(1) == 0)
    def _():
        _na_build_bias(diag_ref, rowbias_ref, bias_ref, patterns)

    def tile(t, carry):
        kb = pl.multiple_of(jnp.clip(NA_Q_ROWS * t - NA_WIN_R // 2, 0, rows - NA_K_ROWS) * GRID_W, tq)
        q0 = pl.multiple_of(t * tq, tq)
        pat = 0
        for i, pid in enumerate(pat_ids):
            pat = jnp.where(t == i, pid, pat)
        q = q_ref[pl.ds(q0, tq), :]
        s_w = _dot_nt(q, k_ref[pl.ds(kb, nk), :]) * scale + bias_ref[pat]
        s_c = _dot_nt(q, kc_ref[...]) * scale
        m = jnp.maximum(jnp.max(s_w, axis=-1, keepdims=True), jnp.max(s_c, axis=-1, keepdims=True))
        p_w = jnp.exp(s_w - m)
        p_c = jnp.exp(s_c - m)
        o = _dot(p_w.astype(BF16), vext_ref[pl.ds(kb, nk), :]) + _dot(p_c.astype(BF16), vcext_ref[...])
        dh = NA_HEAD_DIM
        o_ref[pl.ds(q0, tq), :] = (o[:, :dh] / o[:, dh:dh + 1]).astype(o_ref.dtype)
        return carry

    lax.fori_loop(0, rows // NA_Q_ROWS, tile, 0, unroll=16)


def _na_attention(proj, cproj, rpb, batch, t_len, tc_len, col_q, col_k, col_v):
    rows = t_len // GRID_W
    tq = NA_Q_ROWS * GRID_W
    _, patterns, pat_ids = _na_tables(rows)
    diag = _na_bias_diagonals(rpb)
    n_dr = diag.shape[1]
    dh = NA_HEAD_DIM
    tok = lambda col: pl.BlockSpec((t_len, dh), lambda h, b: (b, col // dh + h))
    return pl.pallas_call(
        functools.partial(_na_kernel, rows=rows, patterns=tuple(patterns), pat_ids=tuple(pat_ids)),
        out_shape=jax.ShapeDtypeStruct((batch * t_len, NA_HEADS * dh), BF16),
        grid=(NA_HEADS, batch),
        in_specs=[tok(col_q), tok(col_k), tok(col_v),
                  pl.BlockSpec((tc_len, dh), lambda h, b: (b, h)),
                  pl.BlockSpec((tc_len, dh), lambda h, b: (b, NA_HEADS + h)),
                  pl.BlockSpec((None, n_dr, LANES), lambda h, b: (h, 0, 0))],
        out_specs=pl.BlockSpec((t_len, dh), lambda h, b: (b, h)),
        scratch_shapes=[pltpu.VMEM((n_dr, GRID_W, LANES), F32),
                        pltpu.VMEM((len(patterns), tq, NA_K_ROWS * GRID_W), F32),
                        pltpu.VMEM((t_len, 2 * dh), BF16), pltpu.VMEM((tc_len, 2 * dh), BF16)],
        compiler_params=_params("arbitrary", "arbitrary"),
        name="na_attention",
    )(proj, proj, proj, cproj, cproj, diag)


def _rope_tables(t_len):
    quarter = RET_DIM // 4
    inv = ROPE_BASE ** (-np.arange(quarter, dtype=np.float64) / quarter)
    tpos = np.arange(t_len)
    row_ang = (tpos // GRID_W).astype(np.float64)[:, None] * inv[None, :]
    col_ang = (tpos % GRID_W).astype(np.float64)[:, None] * inv[None, :]
    cos = np.concatenate([np.cos(row_ang)] * 2 + [np.cos(col_ang)] * 2, axis=-1)
    sin = np.concatenate([-np.sin(row_ang), np.sin(row_ang), -np.sin(col_ang), np.sin(col_ang)], axis=-1)
    return jnp.asarray(cos, F32), jnp.asarray(sin, F32)


def _rope(a, cos, sin):
    half = RET_DIM // 2
    swapped = jnp.concatenate([pltpu.roll(a[:, :half], half // 2, 1), pltpu.roll(a[:, half:], half // 2, 1)], axis=1)
    return a * cos + swapped * sin


def _ret_kernel(dec_ref, q_ref, k_ref, v_ref, gf_ref, gb_ref, kc_ref, vc_ref, cos_ref, sin_ref, o_ref,
                sf_ref, sb_ref, acc_ref, qrot_ref, krot_ref):
    h = pl.program_id(1)
    c = RET_CHUNK
    t_len = q_ref.shape[0]
    tc_len = kc_ref.shape[0]
    nc = t_len // c
    k_scale = RET_DIM ** -0.5

    def log_gamma(direction):
        e = jnp.full((1, 1), dec_ref[direction, h], F32)
        return jnp.log1p(-jnp.exp2(-e))

    lg_f, lg_b = log_gamma(0), log_gamma(1)
    pos = lax.broadcasted_iota(jnp.int32, (c, 1), 0).astype(F32)
    diff = pos - lax.broadcasted_iota(jnp.int32, (1, c), 1).astype(F32)
    dec_f = jnp.where(diff >= 0, jnp.exp(lg_f * jnp.maximum(diff, 0.0)), 0.0) * k_scale
    dec_b = jnp.where(diff <= 0, jnp.exp(lg_b * jnp.maximum(-diff, 0.0)), 0.0) * k_scale
    qdec_f, kdec_f, cdec_f = jnp.exp(lg_f * (pos + 1.0)), jnp.exp(lg_f * (c - 1.0 - pos)) * k_scale, jnp.exp(lg_f * c)
    qdec_b, kdec_b, cdec_b = jnp.exp(lg_b * (c - pos)), jnp.exp(lg_b * pos) * k_scale, jnp.exp(lg_b * c)

    cpos = lax.broadcasted_iota(jnp.int32, (tc_len, 1), 0).astype(F32)
    kc = kc_ref[...].astype(F32) * k_scale
    vc = vc_ref[...]
    sf_ref[...] = _dot_tn((kc * jnp.exp(lg_f * (tc_len - 1.0 - cpos))).astype(BF16), vc)
    sb_ref[...] = _dot_tn((kc * jnp.exp(lg_b * cpos)).astype(BF16), vc)

    def chunk(n, s_ref, dec, qdec, kdec, cdec, g_ref, first_visit):
        rows = pl.ds(pl.multiple_of(n * c, c), c)
        if first_visit:
            cos, sin = cos_ref[rows, :], sin_ref[rows, :]
            qb = _rope(q_ref[rows, :].astype(F32), cos, sin).astype(BF16)
            k = _rope(k_ref[rows, :].astype(F32), cos, sin)
            qrot_ref[rows, :] = qb
            krot_ref[rows, :] = k
        else:
            qb, k = qrot_ref[rows, :], krot_ref[rows, :]
        v = v_ref[rows, :]
        scores = _dot_nt(qb, k.astype(BF16)) * dec
        s = s_ref[...]
        o = _dot(scores.astype(BF16), v) + _dot(qb, s.astype(BF16)) * qdec
        s_ref[...] = s * cdec + _dot_tn((k * kdec).astype(BF16), v)
        on = o * lax.rsqrt(jnp.mean(o * o, axis=-1, keepdims=True) + NORM_EPS)
        gated = _silu(g_ref[rows, :].astype(F32)) * on
        if first_visit:
            acc_ref[rows, :] = gated
        else:
            acc_ref[rows, :] += gated

    assert nc % 2 == 0

    def steps(first_visit):
        def body(n, carry):
            chunk(n, sf_ref, dec_f, qdec_f, kdec_f, cdec_f, gf_ref, first_visit)
            chunk(nc - 1 - n, sb_ref, dec_b, qdec_b, kdec_b, cdec_b, gb_ref, first_visit)
            return carry
        return body

    lax.fori_loop(0, nc // 2, steps(True), 0, unroll=2)
    lax.fori_loop(nc // 2, nc, steps(False), 0, unroll=8)
    o_ref[...] = acc_ref[...].astype(o_ref.dtype)


def _retention(proj, cproj, ret_decay, batch, t_len, tc_len, col_q, col_k, col_v, col_gf, col_gb, ccol_k, ccol_v):
    d = RET_DIM
    cos, sin = _rope_tables(t_len)
    tok = lambda col: pl.BlockSpec((t_len, d), lambda b, h: (b, col // d + h))
    ctx = lambda col: pl.BlockSpec((tc_len, d), lambda b, h: (b, col // d + h))
    tab = pl.BlockSpec((t_len, d), lambda b, h: (0, 0))
    return pl.pallas_call(
        _ret_kernel,
        out_shape=jax.ShapeDtypeStruct((batch * t_len, RET_HEADS * d), BF16),
        grid=(batch, RET_HEADS),
        in_specs=[pl.BlockSpec(memory_space=pltpu.SMEM),
                  tok(col_q), tok(col_k), tok(col_v), tok(col_gf), tok(col_gb), ctx(ccol_k), ctx(ccol_v), tab, tab],
        out_specs=pl.BlockSpec((t_len, d), lambda b, h: (b, h)),
        scratch_shapes=[pltpu.VMEM((d, d), F32), pltpu.VMEM((d, d), F32), pltpu.VMEM((t_len, d), F32),
                        pltpu.VMEM((t_len, d), BF16), pltpu.VMEM((t_len, d), F32)],
        compiler_params=_params("parallel", "arbitrary"),
        name="retention",
    )(ret_decay.astype(F32), proj, proj, proj, proj, proj, cproj, cproj, cos, sin)


def _merge_kernel(ya_ref, yr_ref, wa_ref, wr_ref, ga_ref, gb_ref, o_ref):
    a = _dot(ya_ref[...], wa_ref[...].astype(BF16))
    r = _dot(yr_ref[...], wr_ref[...].astype(BF16))
    o_ref[...] = (jax.nn.sigmoid(ga_ref[...].astype(F32)) * a + jax.nn.sigmoid(gb_ref[...].astype(F32)) * r
                  ).astype(o_ref.dtype)


def _merge(y_na, y_ret, w_na, w_ret, proj, col_ga, col_gb):
    m, ka = y_na.shape
    kr = y_ret.shape[1]
    n = w_na.shape[1]
    tm, tn = 1024, 512
    return pl.pallas_call(
        _merge_kernel,
        out_shape=jax.ShapeDtypeStruct((m, n), BF16),
        grid=(m // tm, n // tn),
        in_specs=[pl.BlockSpec((tm, ka), lambda i, j: (i, 0)),
                  pl.BlockSpec((tm, kr), lambda i, j: (i, 0)),
                  pl.BlockSpec((ka, tn), lambda i, j: (0, j)),
                  pl.BlockSpec((kr, tn), lambda i, j: (0, j)),
                  pl.BlockSpec((tm, tn), lambda i, j: (i, col_ga // tn + j)),
                  pl.BlockSpec((tm, tn), lambda i, j: (i, col_gb // tn + j))],
        out_specs=pl.BlockSpec((tm, tn), lambda i, j: (i, j)),
        compiler_params=_params("parallel", "arbitrary"),
        name="merge",
    )(y_na, y_ret, w_na, w_ret, proj, proj)


def _outproj_kernel(m_ref, w_ref, x_ref, g_ref, o_ref):
    o_ref[...] = x_ref[...] + g_ref[...] * _dot(m_ref[...], w_ref[...].astype(BF16))


def _outproj(mixed, w_out, x2d, mod4, rows_per_sample, k_gate):
    m, k = mixed.shape
    n = w_out.shape[1]
    tm, tn = 1024, 512
    per = rows_per_sample // tm
    return pl.pallas_call(
        _outproj_kernel,
        out_shape=jax.ShapeDtypeStruct((m, n), F32),
        grid=(m // tm, n // tn),
        in_specs=[pl.BlockSpec((tm, k), lambda i, j: (i, 0)),
                  pl.BlockSpec((k, tn), lambda i, j: (0, j)),
                  pl.BlockSpec((tm, tn), lambda i, j: (i, j)),
                  pl.BlockSpec((None, None, 1, tn), lambda i, j: (i // per, k_gate, 0, j))],
        out_specs=pl.BlockSpec((tm, tn), lambda i, j: (i, j)),
        compiler_params=_params("parallel", "arbitrary"),
        name="outproj",
    )(mixed, w_out, x2d, mod4)


def _router_kernel(x_ref, g_ref, sh_ref, sc_ref, wr_ref, o_ref, a_ref):
    d = x_ref.shape[1]
    h = _modulated_norm(x_ref[...], g_ref[...], sh_ref[...], sc_ref[...])
    logits = _dot(h.astype(BF16), wr_ref[...])
    lane = lax.broadcasted_iota(jnp.int32, logits.shape, 1)
    logits = jnp.where(lane < N_EXPERTS, logits, NEG_INF)
    p = jnp.exp(logits - jnp.max(logits, axis=-1, keepdims=True))
    aff = p / jnp.sum(p, axis=-1, keepdims=True)
    o_ref[:, :d] = h
    o_ref[:, d:] = aff
    a_ref[...] = aff[:, :N_EXPERTS]


def _router(x2d, gain, mod4, w_router_pad, rows_per_sample, k_shift):
    r, d = x2d.shape
    tr = 512
    per = rows_per_sample // tr
    return pl.pallas_call(
        _router_kernel,
        out_shape=(jax.ShapeDtypeStruct((r, d + LANES), F32), jax.ShapeDtypeStruct((r, N_EXPERTS), F32)),
        grid=(r // tr,),
        in_specs=[pl.BlockSpec((tr, d), lambda i: (i, 0)),
                  pl.BlockSpec((1, d), lambda i: (0, 0)),
                  pl.BlockSpec((None, None, 1, d), lambda i: (i // per, k_shift, 0, 0)),
                  pl.BlockSpec((None, None, 1, d), lambda i: (i // per, k_shift + 1, 0, 0)),
                  pl.BlockSpec((d, LANES), lambda i: (0, 0))],
        out_specs=(pl.BlockSpec((tr, d + LANES), lambda i: (i, 0)), pl.BlockSpec((tr, N_EXPERTS), lambda i: (i, 0))),
        compiler_params=_params("parallel"),
        name="router",
    )(x2d, gain.reshape(1, d), mod4, mod4, w_router_pad)


TOPK_TILE = 256
GEOMETRIC_STEPS = 32
ARITHMETIC_STEPS = 12


def _topk_kernel(aff_ref, affc_ref, slot_ref, idx_ref, bounds_ref, slot_t_ref, *, cap):
    t_len = aff_ref.shape[0]
    tt = TOPK_TILE
    nt = t_len // tt
    packed = affc_ref[...]

    def per_expert(v):
        shift = LANES // 2
        while shift >= N_EXPERTS:
            v = v + pltpu.roll(v, shift, 1)
            shift //= 2
        return v

    def narrow(c, mid):
        lo, hi = c
        ge = per_expert(jnp.sum(jnp.where(packed >= mid, 1.0, 0.0), axis=0, keepdims=True)) >= cap
        return jnp.where(ge, mid, lo), jnp.where(ge, hi, mid)

    tiny = float(np.finfo(np.float32).tiny)
    above_tiny = per_expert(jnp.sum(jnp.where(packed >= tiny, 1.0, 0.0), axis=0, keepdims=True)) >= cap
    bracket = (jnp.where(above_tiny, tiny, 0.0), jnp.where(above_tiny, 2.0, tiny) + jnp.zeros((1, LANES), F32))
    geometric_mid = lambda c: jnp.clip(jnp.sqrt(c[0]) * jnp.sqrt(c[1]), c[0], c[1])
    bracket = lax.fori_loop(0, GEOMETRIC_STEPS, lambda _, c: narrow(c, geometric_mid(c)), bracket)
    lo, hi = lax.fori_loop(0, ARITHMETIC_STEPS, lambda _, c: narrow(c, 0.5 * (c[0] + c[1])), bracket)

    def count_above(i, cnt):
        r0 = pl.multiple_of(i * tt, tt)
        return cnt + jnp.sum(jnp.where(aff_ref[pl.ds(r0, tt), :] >= hi, 1.0, 0.0), axis=0, keepdims=True)
    need = cap - lax.fori_loop(0, nt, count_above, jnp.zeros((1, LANES), F32))

    tri = jnp.where(lax.broadcasted_iota(jnp.int32, (tt, tt), 0) >= lax.broadcasted_iota(jnp.int32, (tt, tt), 1),
                    1.0, 0.0).astype(BF16)

    def assign(i, carry):
        eq_before, sel_before = carry
        r0 = pl.multiple_of(i * tt, tt)
        a = aff_ref[pl.ds(r0, tt), :]
        above = a >= hi
        tie = (a >= lo) & (a < hi)
        eq = jnp.where(tie, 1.0, 0.0)
        eq_rank = _dot(tri, eq.astype(BF16)) + eq_before
        sel = jnp.where(above | (tie & (eq_rank <= need)), 1.0, 0.0)
        sel_rank = _dot(tri, sel.astype(BF16)) + sel_before
        slot_ref[pl.ds(r0, tt), :] = jnp.where(sel > 0, sel_rank - 1.0, -1.0).astype(jnp.int32)
        bounds_ref[pl.ds(i, 1), :] = sel_before.astype(jnp.int32)
        return (eq_before + jnp.sum(eq, axis=0, keepdims=True), sel_before + jnp.sum(sel, axis=0, keepdims=True))

    zero = jnp.zeros((1, LANES), F32)
    _, total = lax.fori_loop(0, nt, assign, (zero, zero), unroll=2)
    bounds_ref[nt:nt + 1, :] = total.astype(jnp.int32)

    slot_t_ref[...] = jnp.transpose(slot_ref[...].astype(F32))
    idx_ref[...] = jnp.zeros_like(idx_ref)
    sublanes = 8
    tok = lax.broadcasted_iota(jnp.int32, (sublanes, t_len), 1).astype(F32)
    sub = lax.broadcasted_iota(jnp.int32, (sublanes, 1), 0).astype(F32)
    for e in range(N_EXPERTS):
        def body(g, carry):
            s0 = pl.multiple_of(g * sublanes, sublanes)
            hit = slot_t_ref[e:e + 1, :] == sub + s0.astype(F32)
            idx_ref[pl.ds(s0, sublanes), e:e + 1] = jnp.sum(jnp.where(hit, tok, 0.0), axis=1,
                                                            keepdims=True).astype(jnp.int32)
            return carry
        lax.fori_loop(0, cap // sublanes, body, 0, unroll=8)


def _topk(hext, aff, batch, t_len, d, cap):
    nb = t_len // TOPK_TILE + 1
    packed_rows = t_len * N_EXPERTS // LANES
    slot, idx_t, bounds = pl.pallas_call(
        functools.partial(_topk_kernel, cap=cap),
        out_shape=(jax.ShapeDtypeStruct((batch * t_len, LANES), jnp.int32),
                   jax.ShapeDtypeStruct((batch, cap, LANES), jnp.int32),
                   jax.ShapeDtypeStruct((batch, nb, LANES), jnp.int32)),
        grid=(batch,),
        in_specs=[pl.BlockSpec((t_len, LANES), lambda b: (b, d // LANES)),
                  pl.BlockSpec((None, packed_rows, LANES), lambda b: (b, 0, 0))],
        out_specs=(pl.BlockSpec((t_len, LANES), lambda b: (b, 0)),
                   pl.BlockSpec((None, cap, LANES), lambda b: (b, 0, 0)),
                   pl.BlockSpec((None, nb, LANES), lambda b: (b, 0, 0))),
        scratch_shapes=[pltpu.VMEM((LANES, t_len), F32)],
        compiler_params=_params("parallel"),
        name="topk",
    )(hext, aff.reshape(batch, packed_rows, LANES))
    return slot, idx_t[:, :, :N_EXPERTS].transpose(0, 2, 1), bounds


GATHER_ROWS = 256


def _gather_kernel(idx_ref, h_hbm, xe_ref, g_ref, buf, sem, *, t_len, d, nchunk):
    step = pl.program_id(0)
    rc = xe_ref.shape[0]
    cur = step % 2

    def request(st, slot):
        lst, chunk = st // nchunk, st % nchunk
        row0 = (lst // N_EXPERTS) * t_len

        for r in range(rc):
            row = row0 + idx_ref[lst, chunk * rc + r]
            pltpu.make_async_copy(h_hbm.at[pl.ds(row, 1)], buf.at[slot, pl.ds(r, 1)], sem.at[slot]).start(
                priority=r % 2)

    @pl.when(step == 0)
    def _():
        request(0, 0)

    @pl.when(step + 1 < pl.num_programs(0))
    def _():
        request(step + 1, 1 - cur)

    pltpu.make_async_copy(h_hbm.at[pl.ds(0, rc)], buf.at[cur], sem.at[cur]).wait()
    rows = buf[cur]
    xe_ref[...] = rows[:, :d].astype(xe_ref.dtype)
    aff = rows[:, d:]
    e = (step // nchunk) % N_EXPERTS
    lane = lax.broadcasted_iota(jnp.int32, aff.shape, 1)
    g_ref[...] = jnp.broadcast_to(jnp.sum(jnp.where(lane == e, aff, 0.0), axis=1, keepdims=True), aff.shape)


def _gather(idx, hext, batch, t_len, d, cap):
    rc = min(GATHER_ROWS, cap)
    nchunk = cap // rc

    def out_block(s, idx):
        lst = s // nchunk
        return lst % N_EXPERTS, (lst // N_EXPERTS) * nchunk + s % nchunk, 0

    return pl.pallas_call(
        functools.partial(_gather_kernel, t_len=t_len, d=d, nchunk=nchunk),
        out_shape=(jax.ShapeDtypeStruct((N_EXPERTS, batch * cap, d), BF16),
                   jax.ShapeDtypeStruct((N_EXPERTS, batch * cap, LANES), F32)),
        grid_spec=pltpu.PrefetchScalarGridSpec(
            num_scalar_prefetch=1,
            grid=(batch * N_EXPERTS * nchunk,),
            in_specs=[pl.BlockSpec(memory_space=pl.ANY)],
            out_specs=(pl.BlockSpec((None, rc, d), out_block), pl.BlockSpec((None, rc, LANES), out_block)),
            scratch_shapes=[pltpu.VMEM((2, rc, d + LANES), F32), pltpu.SemaphoreType.DMA((2,))]),
        compiler_params=_params("arbitrary"),
        name="gather",
    )(idx.reshape(batch * N_EXPERTS, cap), hext)


def _expert_up_kernel(x_ref, wg_ref, wu_ref, o_ref):
    x = x_ref[...]
    a = _dot(x, wg_ref[...].astype(BF16))
    u = _dot(x, wu_ref[...].astype(BF16))
    o_ref[...] = (_silu(a) * u).astype(o_ref.dtype)


def _expert_up(xe, w_gate, w_up):
    e, m, d = xe.shape
    ff = w_gate.shape[2]
    tf = 256
    return pl.pallas_call(
        _expert_up_kernel,
        out_shape=jax.ShapeDtypeStruct((e, m, ff), BF16),
        grid=(e, ff // tf),
        in_specs=[pl.BlockSpec((None, m, d), lambda i, f: (i, 0, 0)),
                  pl.BlockSpec((None, d, tf), lambda i, f: (i, 0, f)),
                  pl.BlockSpec((None, d, tf), lambda i, f: (i, 0, f))],
        out_specs=pl.BlockSpec((None, m, tf), lambda i, f: (i, 0, f)),
        compiler_params=_params("parallel", "arbitrary"),
        name="expert_up",
    )(xe, w_gate, w_up)


def _expert_down_kernel(a_ref, w_ref, g_ref, o_ref):
    o_ref[...] = (_dot(a_ref[...], w_ref[...].astype(BF16)) * g_ref[:, :1]).astype(o_ref.dtype)


def _expert_down(act, w_down, g):
    e, m, ff = act.shape
    d = w_down.shape[2]
    tn = min(1024, d)
    return pl.pallas_call(
        _expert_down_kernel,
        out_shape=jax.ShapeDtypeStruct((e, m, d), BF16),
        grid=(e, d // tn),
        in_specs=[pl.BlockSpec((None, m, ff), lambda i, j: (i, 0, 0)),
                  pl.BlockSpec((None, ff, tn), lambda i, j: (i, 0, j)),
                  pl.BlockSpec((None, m, LANES), lambda i, j: (i, 0, 0))],
        out_specs=pl.BlockSpec((None, m, tn), lambda i, j: (i, 0, j)),
        compiler_params=_params("parallel", "arbitrary"),
        name="expert_down",
    )(act, w_down, g)


COMBINE_WINDOW = 64
ROW_ALIGN = 16


def _combine_kernel(bounds_ref, slot_ref, ye_hbm, x_ref, g_ref, fn_ref, o_ref, stage, onehot, sem, *, nt, cap):
    step = pl.program_id(0)
    w = COMBINE_WINDOW
    cur = step % 2

    def tile_rows(st, e):
        b, i = st // nt, st % nt
        return b, bounds_ref[b * (nt + 1) + i, e], bounds_ref[b * (nt + 1) + i + 1, e]

    def window(st, e, r):
        b, first, _ = tile_rows(st, e)
        base = (first // ROW_ALIGN) * ROW_ALIGN + r * w
        return b, base, jnp.minimum(base, cap - w)

    def window_copy(st, e, r, buf):
        b, _, src = window(st, e, r)
        return pltpu.make_async_copy(ye_hbm.at[e, pl.ds(pl.multiple_of(b * cap + src, ROW_ALIGN), w), :],
                                     stage.at[buf, pl.ds(e * w, w), :], sem.at[buf])

    def start_round(st, r, buf):
        for e in range(N_EXPERTS):
            window_copy(st, e, r, buf).start()

    def wait_round(st, r, buf):
        for e in range(N_EXPERTS):
            window_copy(st, e, r, buf).wait()

    def scatter(r):
        slots = slot_ref[...]
        per_tile = LANES // w
        lane = lax.broadcasted_iota(jnp.int32, (1, LANES), 1)
        pos = lane % w
        for e0 in range(0, N_EXPERTS, per_tile):
            col, base, src = slots[:, e0:e0 + 1], 0, 0
            for k in range(per_tile):
                _, base_k, src_k = window(step, e0 + k, r)
                mine = lane // w == k
                col = jnp.where(mine, slots[:, e0 + k:e0 + k + 1], col)
                base = jnp.where(mine, base_k, base)
                src = jnp.where(mine, src_k, src)
            hit = (col >= base) & (col - src == pos)
            onehot[:, e0 * w:(e0 + per_tile) * w] = jnp.where(hit, 1.0, 0.0).astype(BF16)
        return _dot(onehot[...], stage[cur])

    @pl.when(step == 0)
    def _():
        start_round(0, 0, 0)

    @pl.when(step + 1 < pl.num_programs(0))
    def _():
        start_round(step + 1, 0, 1 - cur)

    wait_round(step, 0, cur)
    o_ref[...] = scatter(0)

    rounds = 1
    for e in range(N_EXPERTS):
        _, first, last = tile_rows(step, e)
        rounds = jnp.maximum(rounds, (last - (first // ROW_ALIGN) * ROW_ALIGN + w - 1) // w)

    def extra_round(r, carry):
        start_round(step, r, cur)
        wait_round(step, r, cur)
        o_ref[...] += scatter(r)
        return carry

    lax.fori_loop(1, rounds, extra_round, 0)

    v = x_ref[...] + g_ref[...] * o_ref[...]
    y = v * lax.rsqrt(jnp.mean(v * v, axis=-1, keepdims=True) + NORM_EPS)
    o_ref[...] = y * fn_ref[...]


def _combine(bounds, slot, ye, x2d, mod4, final_norm, batch, t_len, cap, k_gate):
    m, d = x2d.shape
    tm = TOPK_TILE
    nt = t_len // tm
    w = COMBINE_WINDOW
    return pl.pallas_call(
        functools.partial(_combine_kernel, nt=nt, cap=cap),
        out_shape=jax.ShapeDtypeStruct((m, d), F32),
        grid_spec=pltpu.PrefetchScalarGridSpec(
            num_scalar_prefetch=1,
            grid=(batch * nt,),
            in_specs=[pl.BlockSpec((tm, LANES), lambda s, bnd: (s, 0)),
                      pl.BlockSpec(memory_space=pl.ANY),
                      pl.BlockSpec((tm, d), lambda s, bnd: (s, 0)),
                      pl.BlockSpec((None, None, 1, d), lambda s, bnd: (s // nt, k_gate, 0, 0)),
                      pl.BlockSpec((1, d), lambda s, bnd: (0, 0))],
            out_specs=pl.BlockSpec((tm, d), lambda s, bnd: (s, 0)),
            scratch_shapes=[pltpu.VMEM((2, N_EXPERTS * w, d), BF16), pltpu.VMEM((tm, N_EXPERTS * w), BF16),
                            pltpu.SemaphoreType.DMA((2,))]),
        compiler_params=_params("arbitrary"),
        name="combine",
    )(bounds.reshape(batch * (nt + 1), LANES), slot, ye, x2d, mod4, final_norm.reshape(1, d))


def kernel(x, c, ctx, c_ctx, norm1, norm2, w_mod, b_mod, w_in, na_rpb, ret_decay, w_branch_na, w_branch_ret,
           w_out, w_router, w_gate, w_up, w_down, final_norm):
    batch, t_len, d = x.shape
    tc_len = ctx.shape[1]
    na_w = NA_HEADS * NA_HEAD_DIM
    ret_w = RET_HEADS * RET_DIM
    col_qa, col_ka, col_va = 0, na_w, 2 * na_w
    col_qr = 3 * na_w
    col_kr, col_vr, col_gf, col_gb = col_qr + ret_w, col_qr + 2 * ret_w, col_qr + 3 * ret_w, col_qr + 4 * ret_w
    col_ga = col_qr + 5 * ret_w
    col_gb2 = col_ga + d
    cap = EC_CAPACITY_FACTOR * t_len // N_EXPERTS
    assert w_in.shape[0] == 1, "single layer"

    x2d = x.reshape(batch * t_len, d)
    cvec = jnp.concatenate([c, c_ctx[None], jnp.zeros((8 - batch - 1, d), F32)], axis=0)
    mod = _modulation(cvec, w_mod[0], b_mod[0])
    mod4 = mod[:batch + 1].reshape(batch + 1, N_MOD, 1, d)

    h = _prenorm(x2d, norm1[0], mod4, t_len, 0, 0)
    hc = _prenorm(ctx.reshape(batch * tc_len, d), norm1[0], mod4, batch * tc_len, batch, 0)
    tn = 512
    proj = _matmul(h, w_in[0], w_in.shape[2], 2048, tn, lambda j: j, "in_proj")
    kv_w = 2 * na_w
    cproj = _matmul(hc, w_in[0], kv_w + 2 * ret_w, batch * tc_len, tn,
                    lambda j: jnp.where(j < kv_w // tn, col_ka // tn + j, col_kr // tn + j - kv_w // tn), "ctx_proj")

    y_na = _na_attention(proj, cproj, na_rpb[0], batch, t_len, tc_len, col_qa, col_ka, col_va)
    y_ret = _retention(proj, cproj, ret_decay[0], batch, t_len, tc_len, col_qr, col_kr, col_vr, col_gf, col_gb,
                       kv_w, kv_w + ret_w)
    mixed = _merge(y_na, y_ret, w_branch_na[0], w_branch_ret[0], proj, col_ga, col_gb2)
    x1 = _outproj(mixed, w_out[0], x2d, mod4, t_len, 2)

    w_router_pad = jnp.pad(w_router[0], ((0, 0), (0, LANES - N_EXPERTS))).astype(BF16)
    hext, aff = _router(x1, norm2[0], mod4, w_router_pad, t_len, 3)
    slot, idx, bounds = _topk(hext, aff, batch, t_len, d, cap)
    xe, g = _gather(idx, hext, batch, t_len, d, cap)
    act = _expert_up(xe, w_gate[0], w_up[0])
    ye = _expert_down(act, w_down[0], g)
    out = _combine(bounds, slot, ye, x1, mod4, final_norm, batch, t_len, cap, 5)
    return out.reshape(batch, t_len, d)
```
